```python
import jax, jax.numpy as jnp
from jax import lax
import numpy as np

D_MODEL = 2048
BATCH = 1
SEQ = 16384
DEPTH = 1

D_MIX = 2 * D_MODEL
D_SSD = 3 * D_MIX // 4
SSD_HEAD_DIM = 64
SSD_HEADS = D_SSD // SSD_HEAD_DIM
SSD_GROUPS = 4
SSD_HEADS_PER_GROUP = SSD_HEADS // SSD_GROUPS
D_STATE = 128
CONV_K = 4
CHUNK = 128
D_CONV = D_SSD + 2 * SSD_GROUPS * D_STATE
D_POOL = D_MIX - D_SSD
POOL_WINDOWS = (2, 4, 8, 16)
POOL_GROUPS = len(POOL_WINDOWS)
POOL_GROUP = D_POOL // POOL_GROUPS
D_IN_PROJ = D_SSD + D_CONV + SSD_HEADS + D_POOL
N_EXPERTS = 32
TOP_K = 4
D_EXPERT = D_MODEL
SWIGLU_LIMIT = 7.0
SWIGLU_ALPHA = 1.702
MOE_BLOCK = 128
NORM_EPS = 1e-6
N_MOD = 6

kernel_name = 'hybrid_ssd_pool_moe_adaln'


def rms_norm(x, g):
    xf = x.astype(jnp.float32)
    y = xf * lax.rsqrt(jnp.mean(xf * xf, axis=-1, keepdims=True) + NORM_EPS)
    return (y * g.astype(jnp.float32)).astype(x.dtype)


def causal_depthwise_conv(u, w, b):
    y = lax.conv_general_dilated(u, w[:, None, :], window_strides=(1,), padding=[(CONV_K - 1, 0)],
                                 dimension_numbers=('NWC', 'WIO', 'NWC'), feature_group_count=u.shape[-1])
    return y + b


def ssd_chunked_scan(x, dt, a, bm, cm):
    b, s, _, p = x.shape
    g, r, n = SSD_GROUPS, SSD_HEADS_PER_GROUP, D_STATE
    nc = s // CHUNK
    xs = jnp.moveaxis(x.reshape(b, nc, CHUNK, g, r, p), 1, 0)
    dts = jnp.moveaxis(dt.reshape(b, nc, CHUNK, g, r), 1, 0)
    das = dts * a.reshape(g, r)
    bs = jnp.moveaxis(bm.reshape(b, nc, CHUNK, g, n), 1, 0)
    cs_ = jnp.moveaxis(cm.reshape(b, nc, CHUNK, g, n), 1, 0)
    causal = jnp.tril(jnp.ones((CHUNK, CHUNK), dtype=bool))[None, :, :, None, None]

    def step(state, inp):
        xc, dtc, dac, bc, cc = inp
        cum = jnp.cumsum(dac, axis=1)
        seg = cum[:, :, None] - cum[:, None, :]
        decay = jnp.exp(jnp.where(causal, seg, -jnp.inf))
        cb = jnp.einsum('blgn,bsgn->blsg', cc, bc)
        m = cb[..., None] * decay * dtc[:, None]
        y_diag = jnp.einsum('blsgr,bsgrp->blgrp', m, xc)
        y_off = jnp.einsum('blgn,bgrpn->blgrp', cc, state) * jnp.exp(cum)[..., None]
        w_end = jnp.exp(cum[:, -1:] - cum) * dtc
        new_state = state * jnp.exp(cum[:, -1])[..., None, None] + jnp.einsum('bsgn,bsgr,bsgrp->bgrpn', bc, w_end, xc)
        return new_state, y_diag + y_off

    state0 = jnp.zeros((b, g, r, p, n), jnp.float32)
    _, ys = lax.scan(step, state0, (xs, dts, das, bs, cs_))
    return jnp.moveaxis(ys, 0, 1).reshape(b, s, SSD_HEADS, p)


def causal_multiscale_pool(u):
    b, s, _ = u.shape
    uf = u.astype(jnp.float32).reshape(b, s, POOL_GROUPS, POOL_GROUP)
    cs = jnp.concatenate([jnp.zeros((b, 1, POOL_GROUPS, POOL_GROUP), jnp.float32), jnp.cumsum(uf, axis=1)], axis=1)
    hi = jnp.arange(1, s + 1)
    outs = []
    for gi, w in enumerate(POOL_WINDOWS):
        csg = cs[:, :, gi]
        lo = jnp.maximum(hi - w, 0)
        win_sum = csg[:, 1:] - csg[:, lo]
        count = jnp.minimum(hi, w).astype(jnp.float32)[None, :, None]
        outs.append(win_sum / count - uf[:, :, gi])
    return jnp.stack(outs, axis=2)


def parallel_mixer(h, w_in_proj, conv_w, conv_b, dt_bias, a_log, d_skip, ssd_norm_g, w_pool, b_pool, pool_scale, w_out_proj):
    b, s, _ = h.shape
    proj = h @ w_in_proj
    z, xbc, dt_raw, u = jnp.split(proj, [D_SSD, D_SSD + D_CONV, D_SSD + D_CONV + SSD_HEADS], axis=-1)
    xbc = jax.nn.silu(causal_depthwise_conv(xbc, conv_w, conv_b))
    xs, bm, cm = jnp.split(xbc, [D_SSD, D_SSD + SSD_GROUPS * D_STATE], axis=-1)
    xs = xs.astype(jnp.float32).reshape(b, s, SSD_HEADS, SSD_HEAD_DIM)
    dt = jax.nn.softplus(dt_raw.astype(jnp.float32) + dt_bias.astype(jnp.float32))
    a = -jnp.exp(a_log.astype(jnp.float32))
    y = ssd_chunked_scan(xs, dt, a,
                         bm.astype(jnp.float32).reshape(b, s, SSD_GROUPS, D_STATE),
                         cm.astype(jnp.float32).reshape(b, s, SSD_GROUPS, D_STATE))
    y = y + d_skip.astype(jnp.float32)[:, None] * xs
    y = y.reshape(b, s, D_SSD) * jax.nn.silu(z.astype(jnp.float32))
    yg = y.reshape(b, s, SSD_GROUPS, D_SSD // SSD_GROUPS)
    yg = yg * lax.rsqrt(jnp.mean(yg * yg, axis=-1, keepdims=True) + NORM_EPS)
    y_ssd = (yg.reshape(b, s, D_SSD) * ssd_norm_g.astype(jnp.float32)).astype(h.dtype)
    pooled = causal_multiscale_pool(u).astype(h.dtype)
    y_pool = jnp.einsum('bsgc,gcd->bsgd', pooled, w_pool).reshape(b, s, D_POOL)
    y_pool = (y_pool + b_pool) * pool_scale
    return jnp.concatenate([y_ssd, y_pool], axis=-1) @ w_out_proj


def moe_block(h, w_router, b_router, w_exp_in, b_exp_in, w_exp_out, b_exp_out):
    b, s, d = h.shape
    t = b * s
    hf = h.reshape(t, d)
    logits = (hf @ w_router).astype(jnp.float32) + b_router.astype(jnp.float32)
    top_val, top_idx = lax.top_k(logits, TOP_K)
    gates = jax.nn.softmax(top_val, axis=-1)
    n_assign = t * TOP_K
    e_flat = top_idx.reshape(-1)
    g_flat = gates.reshape(-1)
    tok_flat = jnp.arange(n_assign, dtype=jnp.int32) // TOP_K
    order = jnp.argsort(e_flat)
    e_sorted = e_flat[order]
    counts = jnp.bincount(e_flat, length=N_EXPERTS)
    padded = (counts + MOE_BLOCK - 1) // MOE_BLOCK * MOE_BLOCK
    ustart = jnp.cumsum(counts) - counts
    pend = jnp.cumsum(padded)
    pstart = pend - padded
    dest = pstart[e_sorted] + jnp.arange(n_assign) - ustart[e_sorted]
    capacity = n_assign + N_EXPERTS * MOE_BLOCK
    n_blocks = capacity // MOE_BLOCK
    row_tok = jnp.zeros((capacity,), jnp.int32).at[dest].set(tok_flat[order])
    row_gate = jnp.zeros((capacity,), jnp.float32).at[dest].set(g_flat[order])
    block_start = jnp.arange(n_blocks) * MOE_BLOCK
    block_expert = jnp.minimum(jnp.searchsorted(pend, block_start, side='right'), N_EXPERTS - 1)

    def expert_rows(args):
        e, tok, gate = args
        xb = hf[tok]
        hb = xb @ w_exp_in[e] + b_exp_in[e]
        glu = jnp.minimum(hb[:, ::2], SWIGLU_LIMIT)
        lin = jnp.clip(hb[:, 1::2], -SWIGLU_LIMIT, SWIGLU_LIMIT)
        act = glu * jax.nn.sigmoid(SWIGLU_ALPHA * glu) * (lin + 1)
        return (act @ w_exp_out[e] + b_exp_out[e]) * gate.astype(h.dtype)[:, None]

    out = lax.map(expert_rows, (block_expert, row_tok.reshape(n_blocks, MOE_BLOCK), row_gate.reshape(n_blocks, MOE_BLOCK)))
    y = jax.ops.segment_sum(out.reshape(capacity, d), row_tok, num_segments=t)
    return y.reshape(b, s, d)


def setup_inputs(seed: int = 0) -> dict:
    key = jax.random.key(seed)
    ks = jax.random.split(key, 24)
    f32 = jnp.float32

    def nrm(k, shape, scale):
        return jax.random.normal(k, shape, f32) * scale

    dt_init = jnp.exp(jax.random.uniform(ks[6], (DEPTH, SSD_HEADS), f32, np.log(1e-3), np.log(1e-1)))
    dt_bias = dt_init + jnp.log(-jnp.expm1(-dt_init))
    a_log = jnp.log(jax.random.uniform(ks[7], (DEPTH, SSD_HEADS), f32, 1.0, 16.0))
    return {
        'x': nrm(ks[0], (BATCH, SEQ, D_MODEL), 1.0),
        'c': nrm(ks[1], (BATCH, D_MODEL), 1.0),
        'w_ada': nrm(ks[2], (DEPTH, D_MODEL, N_MOD * D_MODEL), 0.5 * D_MODEL ** -0.5),
        'b_ada': nrm(ks[3], (DEPTH, N_MOD * D_MODEL), 0.02),
        'norm1_g': 1.0 + nrm(ks[4], (DEPTH, D_MODEL), 0.02),
        'w_in_proj': nrm(ks[5], (DEPTH, D_MODEL, D_IN_PROJ), D_MODEL ** -0.5),
        'conv_w': nrm(ks[8], (DEPTH, CONV_K, D_CONV), CONV_K ** -0.5),
        'conv_b': nrm(ks[9], (DEPTH, D_CONV), 0.02),
        'dt_bias': dt_bias,
        'a_log': a_log,
        'd_skip': 1.0 + nrm(ks[10], (DEPTH, SSD_HEADS), 0.1),
        'ssd_norm_g': 1.0 + nrm(ks[11], (DEPTH, D_SSD), 0.02),
        'w_pool': nrm(ks[12], (DEPTH, POOL_GROUPS, POOL_GROUP, POOL_GROUP), POOL_GROUP ** -0.5),
        'b_pool': nrm(ks[13], (DEPTH, D_POOL), 0.02),
        'pool_scale': 1.0 + nrm(ks[14], (DEPTH, D_POOL), 0.1),
        'w_out_proj': nrm(ks[15], (DEPTH, D_MIX, D_MODEL), D_MIX ** -0.5),
        'norm2_g': 1.0 + nrm(ks[16], (DEPTH, D_MODEL), 0.02),
        'w_router': nrm(ks[17], (DEPTH, D_MODEL, N_EXPERTS), D_MODEL ** -0.5),
        'b_router': nrm(ks[18], (DEPTH, N_EXPERTS), 0.01),
        'w_exp_in': nrm(ks[19], (DEPTH, N_EXPERTS, D_MODEL, 2 * D_EXPERT), D_MODEL ** -0.5),
        'b_exp_in': nrm(ks[20], (DEPTH, N_EXPERTS, 2 * D_EXPERT), 0.02),
        'w_exp_out': nrm(ks[21], (DEPTH, N_EXPERTS, D_EXPERT, D_MODEL), D_EXPERT ** -0.5),
        'b_exp_out': nrm(ks[22], (DEPTH, N_EXPERTS, D_MODEL), 0.02),
        'final_norm_g': 1.0 + nrm(ks[23], (D_MODEL,), 0.02),
    }


def reference(x, c, w_ada, b_ada, norm1_g, w_in_proj, conv_w, conv_b, dt_bias, a_log, d_skip, ssd_norm_g,
              w_pool, b_pool, pool_scale, w_out_proj, norm2_g, w_router, b_router, w_exp_in, b_exp_in,
              w_exp_out, b_exp_out, final_norm_g):
    cond = jax.nn.silu(c)
    for i in range(DEPTH):
        mod = (cond @ w_ada[i] + b_ada[i])[:, None, :]
        sh1, sc1, g1, sh2, sc2, g2 = jnp.split(mod, N_MOD, axis=-1)
        h = rms_norm(x, norm1_g[i]) * (1 + sc1) + sh1
        x = x + g1 * parallel_mixer(h, w_in_proj[i], conv_w[i], conv_b[i], dt_bias[i], a_log[i], d_skip[i],
                                    ssd_norm_g[i], w_pool[i], b_pool[i], pool_scale[i], w_out_proj[i])
        h = rms_norm(x, norm2_g[i]) * (1 + sc2) + sh2
        x = x + g2 * moe_block(h, w_router[i], b_router[i], w_exp_in[i], b_exp_in[i], w_exp_out[i], b_exp_out[i])
    return rms_norm(x, final_norm_g)
```

```python
import functools

import jax
import jax.numpy as jnp
from jax import lax
from jax.experimental import pallas as pl
from jax.experimental.pallas import tpu as pltpu

F32 = jnp.float32
BF16 = jnp.bfloat16
HIGHEST = lax.Precision.HIGHEST

SSD_HEAD_DIM = 64
SSD_GROUPS = 4
D_STATE = 128
CONV_K = 4
CHUNK = 128
POOL_WINDOWS = (2, 4, 8, 16)
TOP_K = 4
SWIGLU_LIMIT = 7.0
SWIGLU_ALPHA = 1.702
NORM_EPS = 1e-6

LANES = 128
SUBLANES = 8
VMEM_LIMIT = 56 * 1024 * 1024


def _cparams(sem, vmem=VMEM_LIMIT):
    return pltpu.CompilerParams(dimension_semantics=sem, vmem_limit_bytes=vmem)


def _silu(v):
    return v * jax.nn.sigmoid(v)


def _ada_kernel(c_ref, w_ref, b_ref, o_ref):
    cond = _silu(c_ref[...])
    cond8 = jnp.broadcast_to(cond, (SUBLANES, cond.shape[1]))
    o = jnp.dot(cond8, w_ref[...], preferred_element_type=F32, precision=HIGHEST)
    o_ref[...] = o[0:1] + b_ref[...]


def _ada(c, w, b):
    d, n = w.shape
    tn = 1536
    return pl.pallas_call(
        _ada_kernel,
        grid=(n // tn,),
        in_specs=[pl.BlockSpec((1, d), lambda j: (0, 0)),
                  pl.BlockSpec((d, tn), lambda j: (0, j)),
                  pl.BlockSpec((1, tn), lambda j: (0, j))],
        out_specs=pl.BlockSpec((1, tn), lambda j: (0, j)),
        out_shape=jax.ShapeDtypeStruct((1, n), F32),
        compiler_params=_cparams(("arbitrary",)),
        name="ada",
    )(c, w, b.reshape(1, n))


def _inproj_kernel(x_ref, g_ref, sc_ref, sh_ref, w_ref, wdt_ref, proj_ref, dt_ref, h_scr):
    @pl.when(pl.program_id(1) == 0)
    def _():
        x = x_ref[...]
        ms = jnp.mean(x * x, axis=-1, keepdims=True)
        h = x * lax.rsqrt(ms + NORM_EPS) * g_ref[...]
        h = h * (1.0 + sc_ref[...]) + sh_ref[...]
        h_scr[...] = h.astype(BF16)
        dt_ref[...] = jnp.dot(h, wdt_ref[...], preferred_element_type=F32, precision=HIGHEST)

    proj_ref[...] = jnp.dot(h_scr[...], w_ref[...], preferred_element_type=F32)


def _in_proj(x, g, sc, sh, w, wdt, tm, tn):
    t, d = x.shape
    n = w.shape[1]
    vec = pl.BlockSpec((1, d), lambda i, j: (0, 0))
    return pl.pallas_call(
        _inproj_kernel,
        grid=(t // tm, n // tn),
        in_specs=[pl.BlockSpec((tm, d), lambda i, j: (i, 0)), vec, vec, vec,
                  pl.BlockSpec((d, tn), lambda i, j: (0, j)),
                  pl.BlockSpec((d, LANES), lambda i, j: (0, 0))],
        out_specs=[pl.BlockSpec((tm, tn), lambda i, j: (i, j)),
                   pl.BlockSpec((tm, LANES), lambda i, j: (i, 0))],
        out_shape=[jax.ShapeDtypeStruct((t, n), F32), jax.ShapeDtypeStruct((t, LANES), F32)],
        scratch_shapes=[pltpu.VMEM((tm, d), BF16)],
        compiler_params=_cparams(("arbitrary", "arbitrary")),
        name="in_proj",
    )(x, g, sc, sh, w, wdt)


def _ssd_pool_kernel(xbc_ref, xbch_ref, z0_ref, z1_ref, z2_ref, u_ref, uh_ref, dt_ref,
                     cw_ref, cb_ref, dtb_ref, alog_ref, dskip_ref, ng_ref, eh_ref,
                     wp_ref, bp_ref, ps_ref, y_ref,
                     st_scr, xe_scr, xa_scr, ue_scr, yd_scr, *, d_ssd, n_heads):
    i = pl.program_id(0)
    L = CHUNK
    gw = d_ssd // SSD_GROUPS
    halo_c = xe_scr.shape[0] - L
    halo_u = ue_scr.shape[0] - L
    first = i == 0

    @pl.when(first)
    def _():
        st_scr[...] = jnp.zeros_like(st_scr)

    xe_scr[0:halo_c, :] = jnp.where(first, 0.0, xbch_ref[...])
    xe_scr[halo_c:, :] = xbc_ref[...]
    ncols = xe_scr.shape[1]
    cblk = 512
    for c0 in range(0, ncols, cblk):
        acc = jnp.broadcast_to(cb_ref[:, c0:c0 + cblk], (L, cblk))
        for k in range(CONV_K):
            tap = xe_scr[pl.ds(halo_c - (CONV_K - 1) + k, L), c0:c0 + cblk]
            acc = acc + cw_ref[k:k + 1, c0:c0 + cblk] * tap
        xa_scr[:, c0:c0 + cblk] = _silu(acc)

    lane = lax.broadcasted_iota(jnp.int32, (L, LANES), 1)
    row = lax.broadcasted_iota(jnp.int32, (L, LANES), 0)
    head_ok = lane < n_heads
    dt = jnp.where(head_ok, jax.nn.softplus(dt_ref[...] + dtb_ref[...]), 0.0)
    a = -jnp.exp(alog_ref[...])
    da = dt * a
    causal = row >= lane
    tri = jnp.where(causal, 1.0, 0.0).astype(F32)
    cum = jnp.dot(tri, da, preferred_element_type=F32, precision=HIGHEST)
    cum_t = cum.T
    dt_t = dt.T
    cum_last = cum[L - 1:L, :]
    ecum = jnp.exp(cum)
    wend = jnp.exp(cum_last - cum) * dt
    eh = eh_ref[...]
    ecum_x = jnp.dot(ecum.astype(BF16), eh, preferred_element_type=F32)
    wend_x = jnp.dot(wend.astype(BF16), eh, preferred_element_type=F32)

    hpg = n_heads // SSD_GROUPS
    for g in range(SSD_GROUPS):
        b_g = xa_scr[:, d_ssd + g * D_STATE: d_ssd + (g + 1) * D_STATE]
        c_g = xa_scr[:, d_ssd + SSD_GROUPS * D_STATE + g * D_STATE:
                     d_ssd + SSD_GROUPS * D_STATE + (g + 1) * D_STATE]
        b_bf = b_g.astype(BF16)
        c_bf = c_g.astype(BF16)
        cb = lax.dot_general(c_bf, b_bf, (((1,), (1,)), ((), ())), preferred_element_type=F32)
        for hp in range(hpg // 2):
            h0 = g * hpg + 2 * hp
            ms = []
            for h in (h0, h0 + 1):
                seg = cum[:, h:h + 1] - cum_t[h:h + 1, :]
                decay = jnp.exp(jnp.where(causal, seg, -jnp.inf))
                ms.append(cb * decay * dt_t[h:h + 1, :])
            lhs = jnp.concatenate(ms, axis=1).astype(BF16)
            c0 = h0 * SSD_HEAD_DIM
            xpair = xa_scr[:, c0:c0 + LANES]
            top = jnp.where(lane < SSD_HEAD_DIM, xpair, 0.0)
            bot = jnp.where(lane >= SSD_HEAD_DIM, xpair, 0.0)
            rhs = jnp.concatenate([top, bot], axis=0).astype(BF16)
            yd_scr[:, c0:c0 + LANES] = jnp.dot(lhs, rhs, preferred_element_type=F32)
        gs = slice(g * gw, (g + 1) * gw)
        st = st_scr[g]
        y_off = jnp.dot(c_bf, st.astype(BF16), preferred_element_type=F32) * ecum_x[:, gs]
        xs_g = xa_scr[:, gs]
        y_g = yd_scr[:, gs] + y_off + dskip_ref[:, gs] * xs_g
        xw = (xs_g * wend_x[:, gs]).astype(BF16)
        st_scr[g] = st * ecum_x[L - 1:L, gs] + jnp.dot(b_g.T.astype(BF16), xw, preferred_element_type=F32)
        zparts = (z0_ref, z1_ref, z2_ref)
        zw = z0_ref.shape[1]
        zg = jnp.concatenate(
            [zparts[(g * gw + o) // zw][:, (g * gw + o) % zw:(g * gw + o) % zw + LANES] for o in range(0, gw, LANES)],
            axis=1)
        y_g = y_g * _silu(zg)
        msq = jnp.mean(y_g * y_g, axis=-1, keepdims=True)
        y_ref[:, gs] = (y_g * lax.rsqrt(msq + NORM_EPS) * ng_ref[:, gs]).astype(y_ref.dtype)

    ue_scr[0:halo_u, :] = jnp.where(first, 0.0, uh_ref[...])
    ue_scr[halo_u:, :] = u_ref[...]
    pg = ue_scr.shape[1] // len(POOL_WINDOWS)
    tpos = (i * L + lax.broadcasted_iota(jnp.int32, (L, pg), 0) + 1).astype(F32)
    for gi, w in enumerate(POOL_WINDOWS):
        cs = slice(gi * pg, (gi + 1) * pg)
        tok = ue_scr[halo_u:, cs]
        win = tok
        for d in range(1, w):
            win = win + ue_scr[pl.ds(halo_u - d, L), cs]
        pooled = win / jnp.minimum(tpos, float(w)) - tok
        yp = jnp.dot(pooled.astype(BF16), wp_ref[gi], preferred_element_type=F32)
        y_ref[:, d_ssd + gi * pg: d_ssd + (gi + 1) * pg] = ((yp + bp_ref[:, cs]) * ps_ref[:, cs]).astype(y_ref.dtype)


def _ssd_pool(proj, dt, cw, cb, dtb, alog, dskip_x, ng, eh, wp, bp, ps, d_ssd, d_conv, d_pool, n_heads):
    t = proj.shape[0]
    L = CHUNK
    halo_c, halo_u = 8, 16
    zw = 1024
    zb = d_conv // zw
    ub = (d_conv + d_ssd) // d_pool
    d_mix = d_ssd + d_pool
    gw = d_ssd // SSD_GROUPS

    def full(shape):
        return pl.BlockSpec(shape, lambda i: (0,) * len(shape))

    in_specs = [
        pl.BlockSpec((L, d_conv), lambda i: (i, 0)),
        pl.BlockSpec((halo_c, d_conv), lambda i: (jnp.maximum(i * (L // halo_c) - 1, 0), 0)),
        pl.BlockSpec((L, zw), lambda i: (i, zb)),
        pl.BlockSpec((L, zw), lambda i: (i, zb + 1)),
        pl.BlockSpec((L, zw), lambda i: (i, zb + 2)),
        pl.BlockSpec((L, d_pool), lambda i: (i, ub)),
        pl.BlockSpec((halo_u, d_pool), lambda i: (jnp.maximum(i * (L // halo_u) - 1, 0), ub)),
        pl.BlockSpec((L, LANES), lambda i: (i, 0)),
        full(cw.shape), full(cb.shape), full(dtb.shape), full(alog.shape), full(dskip_x.shape),
        full(ng.shape), full(eh.shape), full(wp.shape), full(bp.shape), full(ps.shape),
    ]
    kern = functools.partial(_ssd_pool_kernel, d_ssd=d_ssd, n_heads=n_heads)
    return pl.pallas_call(
        kern,
        grid=(t // L,),
        in_specs=in_specs,
        out_specs=pl.BlockSpec((L, d_mix), lambda i: (i, 0)),
        out_shape=jax.ShapeDtypeStruct((t, d_mix), BF16),
        scratch_shapes=[pltpu.VMEM((SSD_GROUPS, D_STATE, gw), F32),
                        pltpu.VMEM((L + halo_c, d_conv), F32),
                        pltpu.VMEM((L, d_conv), F32),
                        pltpu.VMEM((L + halo_u, d_pool), F32),
                        pltpu.VMEM((L, d_ssd), F32)],
        compiler_params=_cparams(("arbitrary",)),
        name="ssd_pool",
    )(proj, proj, proj, proj, proj, proj, proj, dt, cw, cb, dtb, alog, dskip_x, ng, eh, wp, bp, ps)


def _pack_bf16_pairs(lo, hi):
    lo_b = pltpu.bitcast(lo.astype(BF16).astype(F32), jnp.uint32) >> 16
    hi_b = pltpu.bitcast(hi.astype(BF16).astype(F32), jnp.uint32) & jnp.uint32(0xFFFF0000)
    return lo_b | hi_b


def _unpack_bf16_pairs(w):
    lo = pltpu.bitcast(w << 16, F32).astype(BF16)
    hi = pltpu.bitcast(w & jnp.uint32(0xFFFF0000), F32).astype(BF16)
    return lo, hi


def _outproj_kernel(y_ref, x_ref, w_ref, g1_ref, ng_ref, sc_ref, sh_ref, wr_ref, br_ref,
                    x1_ref, h2p_ref, idx_ref, gate_ref, *, n_experts):
    mix = jnp.dot(y_ref[...], w_ref[...], preferred_element_type=F32)
    x1 = x_ref[...] + g1_ref[...] * mix
    x1_ref[...] = x1
    ms = jnp.mean(x1 * x1, axis=-1, keepdims=True)
    h = x1 * lax.rsqrt(ms + NORM_EPS) * ng_ref[...]
    h = h * (1.0 + sc_ref[...]) + sh_ref[...]
    half = h.shape[1] // 2
    h2p_ref[...] = _pack_bf16_pairs(h[:, :half], h[:, half:])

    logits = jnp.dot(h, wr_ref[...], preferred_element_type=F32, precision=HIGHEST) + br_ref[...]
    tm = logits.shape[0]
    lane = lax.broadcasted_iota(jnp.int32, (tm, LANES), 1)
    lane_f = lane.astype(F32)
    vals = jnp.where(lane < n_experts, logits, -jnp.inf)
    top_v, top_i = [], []
    for _ in range(TOP_K):
        m = jnp.max(vals, axis=-1, keepdims=True)
        am = jnp.min(jnp.where(vals == m, lane_f, float(LANES)), axis=-1, keepdims=True)
        top_v.append(m)
        top_i.append(am)
        vals = jnp.where(lane_f == am, -jnp.inf, vals)
    es = [jnp.exp(v - top_v[0]) for v in top_v]
    denom = es[0] + es[1] + es[2] + es[3]
    idx_out = jnp.zeros((tm, LANES), F32)
    gate_out = jnp.zeros((tm, LANES), F32)
    for k in range(TOP_K):
        idx_out = jnp.where(lane == k, top_i[k], idx_out)
        gate_out = jnp.where(lane == k, es[k] / denom, gate_out)
    idx_ref[...] = idx_out.astype(jnp.int32)
    gate_ref[...] = gate_out


def _out_proj(ycat, x, w, g1, ng, sc, sh, wr, br, n_experts, tm):
    t, d = x.shape
    dm = ycat.shape[1]
    vec = pl.BlockSpec((1, d), lambda i: (0, 0))
    kern = functools.partial(_outproj_kernel, n_experts=n_experts)
    return pl.pallas_call(
        kern,
        grid=(t // tm,),
        in_specs=[pl.BlockSpec((tm, dm), lambda i: (i, 0)),
                  pl.BlockSpec((tm, d), lambda i: (i, 0)),
                  pl.BlockSpec((dm, d), lambda i: (0, 0), pipeline_mode=pl.Buffered(1)),
                  vec, vec, vec, vec,
                  pl.BlockSpec((d, LANES), lambda i: (0, 0)),
                  pl.BlockSpec((1, LANES), lambda i: (0, 0))],
        out_specs=[pl.BlockSpec((tm, d), lambda i: (i, 0)),
                   pl.BlockSpec((tm, d // 2), lambda i: (i, 0)),
                   pl.BlockSpec((tm, LANES), lambda i: (i, 0)),
                   pl.BlockSpec((tm, LANES), lambda i: (i, 0))],
        out_shape=[jax.ShapeDtypeStruct((t, d), F32),
                   jax.ShapeDtypeStruct((t, d // 2), jnp.uint32),
                   jax.ShapeDtypeStruct((t, LANES), jnp.int32),
                   jax.ShapeDtypeStruct((t, LANES), F32)],
        compiler_params=_cparams(("arbitrary",)),
        name="out_proj",
    )(ycat, x, w, g1, ng, sc, sh, wr, br)


def _first_half(x, c, w_ada, b_ada, norm1_g, w_in_proj, conv_w, conv_b, dt_bias, a_log, d_skip, ssd_norm_g,
                w_pool, b_pool, pool_scale, w_out_proj, norm2_g, w_router, b_router):
    _, t, d = x.shape
    n_heads = dt_bias.shape[1]
    d_ssd = n_heads * SSD_HEAD_DIM
    d_conv = conv_w.shape[2]
    d_pool = b_pool.shape[1]
    n_experts = w_router.shape[2]
    x2 = x.reshape(t, d)

    mod = _ada(c, w_ada[0], b_ada[0])
    sh1, sc1, g1, sh2, sc2, g2 = [mod[:, k * d:(k + 1) * d] for k in range(6)]

    wi = w_in_proj[0]
    w_main = jnp.concatenate([wi[:, d_ssd:d_ssd + d_conv], wi[:, :d_ssd], wi[:, d_ssd + d_conv + n_heads:]],
                             axis=1).astype(BF16)
    w_dt = jnp.pad(wi[:, d_ssd + d_conv:d_ssd + d_conv + n_heads], ((0, 0), (0, LANES - n_heads)))
    tm1 = min(1024, t)
    proj, dt_raw = _in_proj(x2, norm1_g, sc1, sh1, w_main, w_dt, tm1, 1024)

    pad_h = LANES - n_heads
    dtb = jnp.pad(dt_bias, ((0, 0), (0, pad_h)))
    alog = jnp.pad(a_log, ((0, 0), (0, pad_h)))
    dskip_x = jnp.repeat(d_skip, SSD_HEAD_DIM, axis=1)
    eh = (jnp.arange(LANES)[:, None] == (jnp.arange(d_ssd) // SSD_HEAD_DIM)[None, :]).astype(BF16)
    ycat = _ssd_pool(proj, dt_raw, conv_w[0], conv_b, dtb, alog, dskip_x, ssd_norm_g, eh,
                     w_pool[0].astype(BF16), b_pool, pool_scale, d_ssd, d_conv, d_pool, n_heads)

    wr = jnp.pad(w_router[0], ((0, 0), (0, LANES - n_experts)))
    br = jnp.pad(b_router, ((0, 0), (0, LANES - n_experts)))
    x1, h2p, idx, gates = _out_proj(ycat, x2, w_out_proj[0].astype(BF16), g1, norm2_g, sc2, sh2, wr, br,
                                    n_experts, min(256, t))
    return x1, h2p, idx, gates, g2


MOE_SUB = 256
MOE_ROWS_MAX = 2048
MOE_TF = 256


def _route_tables(idx, n_experts):
    t = idx.shape[0]
    rmax, sub = MOE_ROWS_MAX, MOE_SUB
    ns_max = n_experts + (t * TOP_K + rmax - 1) // rmax
    tok_mask = (idx[:, :, None] == jnp.arange(n_experts, dtype=jnp.int32)[None, None, :]).any(axis=1)
    m = tok_mask.astype(jnp.int32)
    csum = jnp.cumsum(m, axis=0)
    counts = csum[-1]
    rank = jnp.take_along_axis(csum - m, idx, axis=1)
    n_sup = (counts + rmax - 1) // rmax
    per = (counts + jnp.maximum(n_sup, 1) - 1) // jnp.maximum(n_sup, 1)
    rps = jnp.maximum((per + sub - 1) // sub * sub, sub)
    sup_end = jnp.cumsum(n_sup)
    sup_base = sup_end - n_sup
    s_loc = rank // rps[idx]
    flat = (sup_base[idx] + s_loc) * rmax + (rank - s_loc * rps[idx])
    tok = jnp.broadcast_to(jnp.arange(t, dtype=jnp.int32)[:, None], (t, TOP_K))
    dst = jnp.arange(TOP_K, dtype=jnp.int32)[None, :] * t + tok
    n_slots = (ns_max + 1) * rmax
    row_tok = jnp.zeros((n_slots,), jnp.int32).at[flat.reshape(-1)].set(tok.reshape(-1))
    row_dst = jnp.zeros((n_slots,), jnp.int32).at[flat.reshape(-1)].set(dst.reshape(-1))
    s_ids = jnp.arange(ns_max + 1, dtype=jnp.int32)
    st_e = jnp.minimum(jnp.searchsorted(sup_end, s_ids, side="right"), n_experts - 1).astype(jnp.int32)
    st_n = jnp.clip(counts[st_e] - (s_ids - sup_base[st_e]) * rps[st_e], 0, rps[st_e])
    st_n = jnp.where(s_ids < sup_end[-1], st_n, 0).astype(jnp.int32)
    n_super = sup_end[-1:].astype(jnp.int32)
    return (row_tok.reshape(ns_max + 1, rmax), row_dst.reshape(ns_max + 1, rmax), st_e, st_n, n_super)


def _moe_kernel(st_e, st_n, nsup, tok_hbm, dst_hbm, h2p_hbm, win_hbm, wout_hbm, bin_ref, bout_ref, perm_ref,
                y_hbm, tok_s, dst_s, xbuf, acc, win_st, wout_st, wi_bf, wo_bf,
                tsem, gsem, ssem, wsem, *, nf):
    sub, rmax, tf = MOE_SUB, MOE_ROWS_MAX, MOE_TF
    nsub_max = rmax // sub
    dh = xbuf.shape[1]
    n_super = nsup[0]

    def n_sub_of(s):
        return (st_n[s] + sub - 1) // sub

    def table_copies(s, slot):
        return (pltpu.make_async_copy(tok_hbm.at[s], tok_s.at[slot], tsem.at[slot]),
                pltpu.make_async_copy(dst_hbm.at[s], dst_s.at[slot], tsem.at[slot]))

    def weight_copies(e, j, slot):
        c_in = pltpu.make_async_copy(win_hbm.at[e, :, pl.ds(pl.multiple_of(j * 2 * tf, 2 * tf), 2 * tf)],
                                     win_st.at[slot], wsem.at[slot])
        c_out = pltpu.make_async_copy(wout_hbm.at[e, pl.ds(pl.multiple_of(j * tf, tf), tf), :],
                                      wout_st.at[slot], wsem.at[slot])
        return c_in, c_out

    def gather_rows(m, tslot):
        r0 = pl.multiple_of(m * sub, sub)

        def body(i, carry):
            tk = tok_s[tslot, r0 + i]
            pltpu.make_async_copy(h2p_hbm.at[pl.ds(tk, 1)], xbuf.at[pl.ds(r0 + i, 1)], gsem.at[m]).start()
            return carry
        lax.fori_loop(0, sub, body, 0)

    def wait_gather(m):
        r0 = pl.multiple_of(m * sub, sub)
        pltpu.make_async_copy(h2p_hbm.at[pl.ds(0, sub)], xbuf.at[pl.ds(r0, sub)], gsem.at[m]).wait()

    def wait_scatter(m, nvalid):
        r0 = pl.multiple_of(m * sub, sub)

        def body(i, carry):
            pltpu.make_async_copy(acc.at[pl.ds(r0, 1)], y_hbm.at[pl.ds(0, 1)], ssem.at[m]).wait()
            return carry
        lax.fori_loop(0, nvalid, body, 0)

    def valid_rows(n_rows, m):
        return jnp.clip(n_rows - m * sub, 0, sub)

    xbuf[...] = jnp.zeros_like(xbuf)
    acc[...] = jnp.zeros_like(acc)
    for cp in table_copies(0, 0):
        cp.start()
    for cp in table_copies(0, 0):
        cp.wait()

    @pl.when(n_super > 0)
    def _():
        for cp in weight_copies(st_e[0], 0, 0):
            cp.start()
        lax.fori_loop(0, n_sub_of(0), lambda m, c: (gather_rows(m, 0), c)[1], 0)

    def supertile(s, carry):
        e = st_e[s]
        n_rows = st_n[s]
        n_sub = n_sub_of(s)
        tslot = s % 2
        n_rows_prev = jnp.where(s > 0, st_n[jnp.maximum(s - 1, 0)], 0)
        for cp in table_copies(s + 1, 1 - tslot):
            cp.start()
        n_sub_prev = (n_rows_prev + sub - 1) // sub
        lax.fori_loop(n_sub, jnp.maximum(n_sub, n_sub_prev),
                      lambda m, c: (wait_scatter(m, valid_rows(n_rows_prev, m)), c)[1], 0)

        def chunk(j, carry):
            slot = j % 2

            @pl.when(j < nf - 1)
            def _():
                for cp in weight_copies(e, j + 1, 1 - slot):
                    cp.start()

            @pl.when((j == nf - 1) & (s + 1 < n_super))
            def _():
                for cp in weight_copies(st_e[s + 1], 0, 1 - slot):
                    cp.start()

            for cp in weight_copies(e, j, slot):
                cp.wait()
            perm = perm_ref[...]
            rblk = 512
            for cgrp in range(2 * tf // 256):
                for rb in range(0, wi_bf.shape[0], rblk):
                    wblk = win_st[slot, rb:rb + rblk, cgrp * 256:(cgrp + 1) * 256].astype(BF16)
                    pw = jnp.dot(wblk, perm, preferred_element_type=F32).astype(BF16)
                    wi_bf[rb:rb + rblk, cgrp * LANES:(cgrp + 1) * LANES] = pw[:, :LANES]
                    wi_bf[rb:rb + rblk, tf + cgrp * LANES: tf + (cgrp + 1) * LANES] = pw[:, LANES:]
            wo_bf[...] = wout_st[slot].astype(BF16)
            b_in = bin_ref[e, pl.ds(j, 1), :]

            def subtile(m, carry):
                r0 = pl.multiple_of(m * sub, sub)

                @pl.when(j == 0)
                def _():
                    wait_gather(m)
                    wait_scatter(m, valid_rows(n_rows_prev, m))

                x_lo, x_hi = _unpack_bf16_pairs(xbuf[pl.ds(r0, sub), :])
                hb = (jnp.dot(x_lo, wi_bf[0:dh, :], preferred_element_type=F32)
                      + jnp.dot(x_hi, wi_bf[dh:2 * dh, :], preferred_element_type=F32) + b_in)
                glu = jnp.minimum(hb[:, :tf], SWIGLU_LIMIT)
                lin = jnp.clip(hb[:, tf:], -SWIGLU_LIMIT, SWIGLU_LIMIT)
                act = glu * jax.nn.sigmoid(SWIGLU_ALPHA * glu) * (lin + 1.0)
                o = jnp.dot(act.astype(BF16), wo_bf[...], preferred_element_type=F32)
                prev = acc[pl.ds(r0, sub), :]
                acc[pl.ds(r0, sub), :] = jnp.where(j > 0, prev, 0.0) + o
                return carry
            lax.fori_loop(0, n_sub, subtile, 0)
            return carry
        lax.fori_loop(0, nf, chunk, 0)

        for cp in table_copies(s + 1, 1 - tslot):
            cp.wait()
        n_sub_next = n_sub_of(s + 1)
        b_out = bout_ref[e]

        def finish(m, carry):
            r0 = pl.multiple_of(m * sub, sub)
            acc[pl.ds(r0, sub), :] = acc[pl.ds(r0, sub), :] + b_out

            def sc(i, c2):
                d = dst_s[tslot, r0 + i]
                pltpu.make_async_copy(acc.at[pl.ds(r0 + i, 1)], y_hbm.at[pl.ds(d, 1)], ssem.at[m]).start()
                return c2
            lax.fori_loop(0, valid_rows(n_rows, m), sc, 0)

            @pl.when(m < n_sub_next)
            def _():
                gather_rows(m, 1 - tslot)
            return carry
        lax.fori_loop(0, n_sub, finish, 0)
        lax.fori_loop(n_sub, jnp.maximum(n_sub, n_sub_next), lambda m, c: (gather_rows(m, 1 - tslot), c)[1], 0)
        return carry
    lax.fori_loop(0, n_super, supertile, 0)

    n_rows_last = jnp.where(n_super > 0, st_n[jnp.maximum(n_super - 1, 0)], 0)
    for m in range(nsub_max):
        wait_scatter(m, valid_rows(n_rows_last, m))


def _moe(h2p, row_tok, row_dst, st_e, st_n, n_super, w_in, w_out, b_in_c, b_out, perm, t):
    n_experts, d, f2 = w_in.shape
    nf = f2 // (2 * MOE_TF)
    rmax = MOE_ROWS_MAX
    nsub_max = rmax // MOE_SUB
    smem = pl.BlockSpec(memory_space=pltpu.SMEM)
    hbm = pl.BlockSpec(memory_space=pl.ANY)
    vmem = pl.BlockSpec(memory_space=pltpu.VMEM)
    kern = functools.partial(_moe_kernel, nf=nf)
    return pl.pallas_call(
        kern,
        in_specs=[smem, smem, smem, hbm, hbm, hbm, hbm, hbm, vmem, vmem, vmem],
        out_specs=hbm,
        out_shape=jax.ShapeDtypeStruct((TOP_K * t, d), F32),
        scratch_shapes=[pltpu.SMEM((2, rmax), jnp.int32),
                        pltpu.SMEM((2, rmax), jnp.int32),
                        pltpu.VMEM((rmax, d // 2), jnp.uint32),
                        pltpu.VMEM((rmax, d), F32),
                        pltpu.VMEM((2, d, 2 * MOE_TF), F32),
                        pltpu.VMEM((2, MOE_TF, d), F32),
                        pltpu.VMEM((d, 2 * MOE_TF), BF16),
                        pltpu.VMEM((MOE_TF, d), BF16),
                        pltpu.SemaphoreType.DMA((2,)),
                        pltpu.SemaphoreType.DMA((nsub_max,)),
                        pltpu.SemaphoreType.DMA((nsub_max,)),
                        pltpu.SemaphoreType.DMA((2,))],
        compiler_params=pltpu.CompilerParams(vmem_limit_bytes=VMEM_LIMIT, has_side_effects=True),
        name="moe",
    )(st_e, st_n, n_super, row_tok, row_dst, h2p, w_in, w_out, b_in_c, b_out, perm)


def _combine_kernel(y0_ref, y1_ref, y2_ref, y3_ref, gate_ref, x1_ref, g2_ref, fg_ref, o_ref):
    g = gate_ref[...]
    y = (y0_ref[...] * g[:, 0:1] + y1_ref[...] * g[:, 1:2]) + (y2_ref[...] * g[:, 2:3] + y3_ref[...] * g[:, 3:4])
    x2 = x1_ref[...] + g2_ref[...] * y
    ms = jnp.mean(x2 * x2, axis=-1, keepdims=True)
    o_ref[...] = x2 * lax.rsqrt(ms + NORM_EPS) * fg_ref[...]


def _combine(y, gates, x1, g2, fg, tm):
    t, d = x1.shape
    nb = t // tm
    vec = pl.BlockSpec((1, d), lambda i: (0, 0))
    yspecs = [pl.BlockSpec((tm, d), functools.partial(lambda i, k: (k * nb + i, 0), k=k)) for k in range(TOP_K)]
    return pl.pallas_call(
        _combine_kernel,
        grid=(nb,),
        in_specs=yspecs + [pl.BlockSpec((tm, LANES), lambda i: (i, 0)),
                           pl.BlockSpec((tm, d), lambda i: (i, 0)), vec, vec],
        out_specs=pl.BlockSpec((tm, d), lambda i: (i, 0)),
        out_shape=jax.ShapeDtypeStruct((t, d), F32),
        compiler_params=_cparams(("arbitrary",)),
        name="combine",
    )(y, y, y, y, gates, x1, g2, fg)


def kernel(x, c, w_ada, b_ada, norm1_g, w_in_proj, conv_w, conv_b, dt_bias, a_log, d_skip, ssd_norm_g, w_pool,
           b_pool, pool_scale, w_out_proj, norm2_g, w_router, b_router, w_exp_in, b_exp_in, w_exp_out, b_exp_out,
           final_norm_g):
    assert x.shape[0] == 1 and w_ada.shape[0] == 1
    _, t, d = x.shape
    n_experts = w_router.shape[2]
    x1, h2p, idx, gates, g2 = _first_half(x, c, w_ada, b_ada, norm1_g, w_in_proj, conv_w, conv_b, dt_bias, a_log,
                                          d_skip, ssd_norm_g, w_pool, b_pool, pool_scale, w_out_proj, norm2_g,
                                          w_router, b_router)
    row_tok, row_dst, st_e, st_n, n_super = _route_tables(idx[:, :TOP_K], n_experts)
    f = w_exp_out.shape[2]
    nf = f // MOE_TF
    b_in = b_exp_in[0].reshape(n_experts, nf, MOE_TF, 2)
    b_in_c = jnp.concatenate([b_in[..., 0], b_in[..., 1]], axis=-1)
    src = jnp.concatenate([2 * jnp.arange(LANES), 2 * jnp.arange(LANES) + 1])
    perm = (jnp.arange(2 * LANES)[:, None] == src[None, :]).astype(BF16)
    y = _moe(h2p, row_tok, row_dst, st_e, st_n, n_super, w_exp_in.reshape(w_exp_in.shape[1:]),
             w_exp_out.reshape(w_exp_out.shape[1:]), b_in_c, b_exp_out.reshape(n_experts, 1, d), perm, t)
    out = _combine(y, gates, x1, g2, final_norm_g.reshape(1, d), min(256, t))
    return out.reshape(x.shape)
```

```python
import functools

import jax
import jax.numpy as jnp
from jax import lax
from jax.experimental import pallas as pl
from jax.experimental.pallas import tpu as pltpu

F32 = jnp.float32
BF16 = jnp.bfloat16
HIGHEST = lax.Precision.HIGHEST

SSD_HEAD_DIM = 64
SSD_GROUPS = 4
D_STATE = 128
CONV_K = 4
CHUNK = 128
POOL_WINDOWS = (2, 4, 8, 16)
TOP_K = 4
SWIGLU_LIMIT = 7.0
SWIGLU_ALPHA = 1.702
NORM_EPS = 1e-6

LANES = 128
SUBLANES = 8
VMEM_LIMIT = 56 * 1024 * 1024


def _cparams(sem, vmem=VMEM_LIMIT):
    return pltpu.CompilerParams(dimension_semantics=sem, vmem_limit_bytes=vmem)


def _silu(v):
    return v * jax.nn.sigmoid(v)


def _ada_kernel(c_ref, w_ref, b_ref, o_ref):
    cond = _silu(c_ref[...])
    cond8 = jnp.broadcast_to(cond, (SUBLANES, cond.shape[1]))
    o = jnp.dot(cond8, w_ref[...], preferred_element_type=F32, precision=HIGHEST)
    o_ref[...] = o[0:1] + b_ref[...]


def _ada(c, w, b):
    d, n = w.shape
    tn = 1536
    return pl.pallas_call(
        _ada_kernel,
        grid=(n // tn,),
        in_specs=[pl.BlockSpec((1, d), lambda j: (0, 0)),
                  pl.BlockSpec((d, tn), lambda j: (0, j)),
                  pl.BlockSpec((1, tn), lambda j: (0, j))],
        out_specs=pl.BlockSpec((1, tn), lambda j: (0, j)),
        out_shape=jax.ShapeDtypeStruct((1, n), F32),
        compiler_params=_cparams(("arbitrary",)),
        name="ada",
    )(c, w, b.reshape(1, n))


def _inproj_kernel(x_ref, g_ref, sc_ref, sh_ref, w_ref, wdt_ref, proj_ref, dt_ref, h_scr):
    @pl.when(pl.program_id(1) == 0)
    def _():
        x = x_ref[...]
        ms = jnp.mean(x * x, axis=-1, keepdims=True)
        h = x * lax.rsqrt(ms + NORM_EPS) * g_ref[...]
        h = h * (1.0 + sc_ref[...]) + sh_ref[...]
        h_scr[...] = h.astype(BF16)
        dt_ref[...] = jnp.dot(h, wdt_ref[...], preferred_element_type=F32, precision=HIGHEST)

    proj_ref[...] = jnp.dot(h_scr[...], w_ref[...], preferred_element_type=F32)


def _in_proj(x, g, sc, sh, w, wdt, tm, tn):
    t, d = x.shape
    n = w.shape[1]
    vec = pl.BlockSpec((1, d), lambda i, j: (0, 0))
    return pl.pallas_call(
        _inproj_kernel,
        grid=(t // tm, n // tn),
        in_specs=[pl.BlockSpec((tm, d), lambda i, j: (i, 0)), vec, vec, vec,
                  pl.BlockSpec((d, tn), lambda i, j: (0, j)),
                  pl.BlockSpec((d, LANES), lambda i, j: (0, 0))],
        out_specs=[pl.BlockSpec((tm, tn), lambda i, j: (i, j)),
                   pl.BlockSpec((tm, LANES), lambda i, j: (i, 0))],
        out_shape=[jax.ShapeDtypeStruct((t, n), F32), jax.ShapeDtypeStruct((t, LANES), F32)],
        scratch_shapes=[pltpu.VMEM((tm, d), BF16)],
        compiler_params=_cparams(("arbitrary", "arbitrary")),
        name="in_proj",
    )(x, g, sc, sh, w, wdt)


def _ssd_pool_kernel(xbc_ref, xbch_ref, z0_ref, z1_ref, z2_ref, u_ref, uh_ref, dt_ref,
                     cw_ref, cb_ref, dtb_ref, alog_ref, dskip_ref, ng_ref, eh_ref,
                     wp_ref, bp_ref, ps_ref, y_ref,
                     st_scr, xe_scr, xa_scr, ue_scr, yd_scr, *, d_ssd, n_heads):
    i = pl.program_id(0)
    L = CHUNK
    gw = d_ssd // SSD_GROUPS
    halo_c = xe_scr.shape[0] - L
    halo_u = ue_scr.shape[0] - L
    first = i == 0

    @pl.when(first)
    def _():
        st_scr[...] = jnp.zeros_like(st_scr)

    xe_scr[0:halo_c, :] = jnp.where(first, 0.0, xbch_ref[...])
    xe_scr[halo_c:, :] = xbc_ref[...]
    ncols = xe_scr.shape[1]
    cblk = 512
    for c0 in range(0, ncols, cblk):
        acc = jnp.broadcast_to(cb_ref[:, c0:c0 + cblk], (L, cblk))
        for k in range(CONV_K):
            tap = xe_scr[pl.ds(halo_c - (CONV_K - 1) + k, L), c0:c0 + cblk]
            acc = acc + cw_ref[k:k + 1, c0:c0 + cblk] * tap
        xa_scr[:, c0:c0 + cblk] = _silu(acc)

    lane = lax.broadcasted_iota(jnp.int32, (L, LANES), 1)
    row = lax.broadcasted_iota(jnp.int32, (L, LANES), 0)
    head_ok = lane < n_heads
    dt = jnp.where(head_ok, jax.nn.softplus(dt_ref[...] + dtb_ref[...]), 0.0)
    a = -jnp.exp(alog_ref[...])
    da = dt * a
    causal = row >= lane
    tri = jnp.where(causal, 1.0, 0.0).astype(F32)
    cum = jnp.dot(tri, da, preferred_element_type=F32, precision=HIGHEST)
    cum_t = cum.T
    dt_t = dt.T
    cum_last = cum[L - 1:L, :]
    ecum = jnp.exp(cum)
    wend = jnp.exp(cum_last - cum) * dt
    eh = eh_ref[...]
    ecum_x = jnp.dot(ecum.astype(BF16), eh, preferred_element_type=F32)
    wend_x = jnp.dot(wend.astype(BF16), eh, preferred_element_type=F32)

    hpg = n_heads // SSD_GROUPS
    for g in range(SSD_GROUPS):
        b_g = xa_scr[:, d_ssd + g * D_STATE: d_ssd + (g + 1) * D_STATE]
        c_g = xa_scr[:, d_ssd + SSD_GROUPS * D_STATE + g * D_STATE:
                     d_ssd + SSD_GROUPS * D_STATE + (g + 1) * D_STATE]
        b_bf = b_g.astype(BF16)
        c_bf = c_g.astype(BF16)
        cb = lax.dot_general(c_bf, b_bf, (((1,), (1,)), ((), ())), preferred_element_type=F32)
        for hp in range(hpg // 2):
            h0 = g * hpg + 2 * hp
            ms = []
            for h in (h0, h0 + 1):
                seg = cum[:, h:h + 1] - cum_t[h:h + 1, :]
                decay = jnp.exp(jnp.where(causal, seg, -jnp.inf))
                ms.append(cb * decay * dt_t[h:h + 1, :])
            lhs = jnp.concatenate(ms, axis=1).astype(BF16)
            c0 = h0 * SSD_HEAD_DIM
            xpair = xa_scr[:, c0:c0 + LANES]
            top = jnp.where(lane < SSD_HEAD_DIM, xpair, 0.0)
            bot = jnp.where(lane >= SSD_HEAD_DIM, xpair, 0.0)
            rhs = jnp.concatenate([top, bot], axis=0).astype(BF16)
            yd_scr[:, c0:c0 + LANES] = jnp.dot(lhs, rhs, preferred_element_type=F32)
        gs = slice(g * gw, (g + 1) * gw)
        st = st_scr[g]
        y_off = jnp.dot(c_bf, st.astype(BF16), preferred_element_type=F32) * ecum_x[:, gs]
        xs_g = xa_scr[:, gs]
        y_g = yd_scr[:, gs] + y_off + dskip_ref[:, gs] * xs_g
        xw = (xs_g * wend_x[:, gs]).astype(BF16)
        st_scr[g] = st * ecum_x[L - 1:L, gs] + jnp.dot(b_g.T.astype(BF16), xw, preferred_element_type=F32)
        zparts = (z0_ref, z1_ref, z2_ref)
        zw = z0_ref.shape[1]
        zg = jnp.concatenate(
            [zparts[(g * gw + o) // zw][:, (g * gw + o) % zw:(g * gw + o) % zw + LANES] for o in range(0, gw, LANES)],
            axis=1)
        y_g = y_g * _silu(zg)
        msq = jnp.mean(y_g * y_g, axis=-1, keepdims=True)
        y_ref[:, gs] = (y_g * lax.rsqrt(msq + NORM_EPS) * ng_ref[:, gs]).astype(y_ref.dtype)

    ue_scr[0:halo_u, :] = jnp.where(first, 0.0, uh_ref[...])
    ue_scr[halo_u:, :] = u_ref[...]
    pg = ue_scr.shape[1] // len(POOL_WINDOWS)
    tpos = (i * L + lax.broadcasted_iota(jnp.int32, (L, pg), 0) + 1).astype(F32)
    for gi, w in enumerate(POOL_WINDOWS):
        cs = slice(gi * pg, (gi + 1) * pg)
        tok = ue_scr[halo_u:, cs]
        win = tok
        for d in range(1, w):
            win = win + ue_scr[pl.ds(halo_u - d, L), cs]
        pooled = win / jnp.minimum(tpos, float(w)) - tok
        yp = jnp.dot(pooled.astype(BF16), wp_ref[gi], preferred_element_type=F32)
        y_ref[:, d_ssd + gi * pg: d_ssd + (gi + 1) * pg] = ((yp + bp_ref[:, cs]) * ps_ref[:, cs]).astype(y_ref.dtype)


def _ssd_pool(proj, dt, cw, cb, dtb, alog, dskip_x, ng, eh, wp, bp, ps, d_ssd, d_conv, d_pool, n_heads):
    t = proj.shape[0]
    L = CHUNK
    halo_c, halo_u = 8, 16
    zw = 1024
    zb = d_conv // zw
    ub = (d_conv + d_ssd) // d_pool
    d_mix = d_ssd + d_pool
    gw = d_ssd // SSD_GROUPS

    def full(shape):
        return pl.BlockSpec(shape, lambda i: (0,) * len(shape))

    in_specs = [
        pl.BlockSpec((L, d_conv), lambda i: (i, 0)),
        pl.BlockSpec((halo_c, d_conv), lambda i: (jnp.maximum(i * (L // halo_c) - 1, 0), 0)),
        pl.BlockSpec((L, zw), lambda i: (i, zb)),
        pl.BlockSpec((L, zw), lambda i: (i, zb + 1)),
        pl.BlockSpec((L, zw), lambda i: (i, zb + 2)),
        pl.BlockSpec((L, d_pool), lambda i: (i, ub)),
        pl.BlockSpec((halo_u, d_pool), lambda i: (jnp.maximum(i * (L // halo_u) - 1, 0), ub)),
        pl.BlockSpec((L, LANES), lambda i: (i, 0)),
        full(cw.shape), full(cb.shape), full(dtb.shape), full(alog.shape), full(dskip_x.shape),
        full(ng.shape), full(eh.shape), full(wp.shape), full(bp.shape), full(ps.shape),
    ]
    kern = functools.partial(_ssd_pool_kernel, d_ssd=d_ssd, n_heads=n_heads)
    return pl.pallas_call(
        kern,
        grid=(t // L,),
        in_specs=in_specs,
        out_specs=pl.BlockSpec((L, d_mix), lambda i: (i, 0)),
        out_shape=jax.ShapeDtypeStruct((t, d_mix), BF16),
        scratch_shapes=[pltpu.VMEM((SSD_GROUPS, D_STATE, gw), F32),
                        pltpu.VMEM((L + halo_c, d_conv), F32),
                        pltpu.VMEM((L, d_conv), F32),
                        pltpu.VMEM((L + halo_u, d_pool), F32),
                        pltpu.VMEM((L, d_ssd), F32)],
        compiler_params=_cparams(("arbitrary",)),
        name="ssd_pool",
    )(proj, proj, proj, proj, proj, proj, proj, dt, cw, cb, dtb, alog, dskip_x, ng, eh, wp, bp, ps)


def _pack_bf16_pairs(lo, hi):
    lo_b = pltpu.bitcast(lo.astype(BF16).astype(F32), jnp.uint32) >> 16
    hi_b = pltpu.bitcast(hi.astype(BF16).astype(F32), jnp.uint32) & jnp.uint32(0xFFFF0000)
    return lo_b | hi_b


def _unpack_bf16_pairs(w):
    lo = pltpu.bitcast(w << 16, F32).astype(BF16)
    hi = pltpu.bitcast(w & jnp.uint32(0xFFFF0000), F32).astype(BF16)
    return lo, hi


def _outproj_kernel(y_ref, x_ref, w_ref, g1_ref, ng_ref, sc_ref, sh_ref, wr_ref, br_ref,
                    x1_ref, h2p_ref, idx_ref, gate_ref, *, n_experts):
    mix = jnp.dot(y_ref[...], w_ref[...], preferred_element_type=F32)
    x1 = x_ref[...] + g1_ref[...] * mix
    x1_ref[...] = x1
    ms = jnp.mean(x1 * x1, axis=-1, keepdims=True)
    h = x1 * lax.rsqrt(ms + NORM_EPS) * ng_ref[...]
    h = h * (1.0 + sc_ref[...]) + sh_ref[...]
    half = h.shape[1] // 2
    h2p_ref[...] = _pack_bf16_pairs(h[:, :half], h[:, half:])

    h_hi = h.astype(BF16)
    h_mid = (h - h_hi.astype(F32)).astype(BF16)
    wr = wr_ref[...]
    r_hi = jnp.dot(h_hi, wr, preferred_element_type=F32)
    r_mid = jnp.dot(h_mid, wr, preferred_element_type=F32)
    logits = (r_hi + r_mid) + pltpu.roll(r_hi, LANES - n_experts, axis=1) + br_ref[...]
    tm = logits.shape[0]
    lane = lax.broadcasted_iota(jnp.int32, (tm, LANES), 1)
    lane_f = lane.astype(F32)
    vals = jnp.where(lane < n_experts, logits, -jnp.inf)
    top_v, top_i = [], []
    for _ in range(TOP_K):
        m = jnp.max(vals, axis=-1, keepdims=True)
        am = jnp.min(jnp.where(vals == m, lane_f, float(LANES)), axis=-1, keepdims=True)
        top_v.append(m)
        top_i.append(am)
        vals = jnp.where(lane_f == am, -jnp.inf, vals)
    es = [jnp.exp(v - top_v[0]) for v in top_v]
    denom = es[0] + es[1] + es[2] + es[3]
    idx_out = jnp.zeros((tm, LANES), F32)
    gate_out = jnp.zeros((tm, LANES), F32)
    for k in range(TOP_K):
        idx_out = jnp.where(lane == k, top_i[k], idx_out)
        gate_out = jnp.where(lane == k, es[k] / denom, gate_out)
    idx_ref[...] = idx_out.astype(jnp.int32)
    gate_ref[...] = gate_out


def _out_proj(ycat, x, w, g1, ng, sc, sh, wr, br, n_experts, tm):
    t, d = x.shape
    dm = ycat.shape[1]
    vec = pl.BlockSpec((1, d), lambda i: (0, 0))
    kern = functools.partial(_outproj_kernel, n_experts=n_experts)
    return pl.pallas_call(
        kern,
        grid=(t // tm,),
        in_specs=[pl.BlockSpec((tm, dm), lambda i: (i, 0)),
                  pl.BlockSpec((tm, d), lambda i: (i, 0)),
                  pl.BlockSpec((dm, d), lambda i: (0, 0), pipeline_mode=pl.Buffered(1)),
                  vec, vec, vec, vec,
                  pl.BlockSpec((d, LANES), lambda i: (0, 0)),
                  pl.BlockSpec((1, LANES), lambda i: (0, 0))],
        out_specs=[pl.BlockSpec((tm, d), lambda i: (i, 0)),
                   pl.BlockSpec((tm, d // 2), lambda i: (i, 0)),
                   pl.BlockSpec((tm, LANES), lambda i: (i, 0)),
                   pl.BlockSpec((tm, LANES), lambda i: (i, 0))],
        out_shape=[jax.ShapeDtypeStruct((t, d), F32),
                   jax.ShapeDtypeStruct((t, d // 2), jnp.uint32),
                   jax.ShapeDtypeStruct((t, LANES), jnp.int32),
                   jax.ShapeDtypeStruct((t, LANES), F32)],
        compiler_params=_cparams(("arbitrary",)),
        name="out_proj",
    )(ycat, x, w, g1, ng, sc, sh, wr, br)


def _first_half(x, c, w_ada, b_ada, norm1_g, w_in_proj, conv_w, conv_b, dt_bias, a_log, d_skip, ssd_norm_g,
                w_pool, b_pool, pool_scale, w_out_proj, norm2_g, w_router, b_router):
    _, t, d = x.shape
    n_heads = dt_bias.shape[1]
    d_ssd = n_heads * SSD_HEAD_DIM
    d_conv = conv_w.shape[2]
    d_pool = b_pool.shape[1]
    n_experts = w_router.shape[2]
    x2 = x.reshape(t, d)

    mod = _ada(c, w_ada[0], b_ada[0])
    sh1, sc1, g1, sh2, sc2, g2 = [mod[:, k * d:(k + 1) * d] for k in range(6)]

    wi = w_in_proj[0]
    w_main = jnp.concatenate([wi[:, d_ssd:d_ssd + d_conv], wi[:, :d_ssd], wi[:, d_ssd + d_conv + n_heads:]],
                             axis=1).astype(BF16)
    w_dt = jnp.pad(wi[:, d_ssd + d_conv:d_ssd + d_conv + n_heads], ((0, 0), (0, LANES - n_heads)))
    tm1 = min(1024, t)
    proj, dt_raw = _in_proj(x2, norm1_g, sc1, sh1, w_main, w_dt, tm1, 1024)

    pad_h = LANES - n_heads
    dtb = jnp.pad(dt_bias, ((0, 0), (0, pad_h)))
    alog = jnp.pad(a_log, ((0, 0), (0, pad_h)))
    dskip_x = jnp.repeat(d_skip, SSD_HEAD_DIM, axis=1)
    eh = (jnp.arange(LANES)[:, None] == (jnp.arange(d_ssd) // SSD_HEAD_DIM)[None, :]).astype(BF16)
    ycat = _ssd_pool(proj, dt_raw, conv_w[0], conv_b, dtb, alog, dskip_x, ssd_norm_g, eh,
                     w_pool[0].astype(BF16), b_pool, pool_scale, d_ssd, d_conv, d_pool, n_heads)

    wr_hi = w_router[0].astype(BF16)
    wr_mid = (w_router[0] - wr_hi.astype(F32)).astype(BF16)
    assert 2 * n_experts <= LANES
    wr = jnp.pad(jnp.concatenate([wr_hi, wr_mid], axis=1), ((0, 0), (0, LANES - 2 * n_experts)))
    br = jnp.pad(b_router, ((0, 0), (0, LANES - n_experts)))
    x1, h2p, idx, gates = _out_proj(ycat, x2, w_out_proj[0].astype(BF16), g1, norm2_g, sc2, sh2, wr, br,
                                    n_experts, min(512, t))
    return x1, h2p, idx, gates, g2


MOE_SUB = 256
MOE_ROWS_MAX = 2304
MOE_TF = 256
MOE_TAB_ALIGN = 1024
MOE_TAB = 4096


def _route_tables(idx, n_experts):
    t, k = idx.shape
    rmax, sub = MOE_ROWS_MAX, MOE_SUB
    ns_max = n_experts + (t * k + rmax - 1) // rmax
    tok = jnp.arange(t, dtype=jnp.int32)[:, None]
    key = (idx * t + tok) * k + jnp.arange(k, dtype=jnp.int32)[None, :]
    skey = jnp.sort(key.reshape(-1))
    row_tok = (skey // k) % t
    row_dst = (skey % k) * t + row_tok
    bounds = jnp.searchsorted(skey, jnp.arange(n_experts + 1, dtype=jnp.int32) * (t * k)).astype(jnp.int32)
    off, counts = bounds[:-1], bounds[1:] - bounds[:-1]
    n_sup = (counts + rmax - 1) // rmax
    per = (counts + jnp.maximum(n_sup, 1) - 1) // jnp.maximum(n_sup, 1)
    rps = jnp.maximum((per + sub - 1) // sub * sub, sub)
    sup_end = jnp.cumsum(n_sup)
    sup_base = sup_end - n_sup
    s_ids = jnp.arange(ns_max + 1, dtype=jnp.int32)
    live = s_ids < sup_end[-1]
    st_e = jnp.minimum(jnp.searchsorted(sup_end, s_ids, side="right"), n_experts - 1).astype(jnp.int32)
    s_loc = s_ids - sup_base[st_e]
    st_n = jnp.where(live, jnp.clip(counts[st_e] - s_loc * rps[st_e], 0, rps[st_e]), 0).astype(jnp.int32)
    st_start = jnp.where(live, off[st_e] + s_loc * rps[st_e], 0).astype(jnp.int32)
    n_super = sup_end[-1:].astype(jnp.int32)
    row_tok = jnp.pad(row_tok, (0, MOE_TAB))
    row_dst = jnp.pad(row_dst, (0, MOE_TAB))
    return row_tok, row_dst, st_e, st_n, st_start, n_super


def _moe_kernel(st_e, st_n, st_start, nsup, tok_hbm, dst_hbm, h2p_hbm, win_hbm, wout_hbm, bin_ref, bout_ref,
                perm_ref, y_hbm, tok_s, dst_s, xbuf, acc, win_st, wout_st, wi_bf, wo_bf, act_scr,
                tsem, gsem, ssem, wsem, *, nf):
    sub, rmax, tf = MOE_SUB, MOE_ROWS_MAX, MOE_TF
    nsub_max = rmax // sub
    gsub = sub // SUBLANES
    dh = xbuf.shape[2]
    d_out = acc.shape[2]
    n_super = nsup[0]

    def n_sub_of(s):
        return (st_n[s] + sub - 1) // sub

    def tab_base(s):
        return pl.multiple_of((st_start[s] // MOE_TAB_ALIGN) * MOE_TAB_ALIGN, MOE_TAB_ALIGN)

    def tab_off(s, slot):
        return slot * MOE_TAB + st_start[s] - tab_base(s)

    def table_copies(s, slot):
        win = pl.ds(tab_base(s), MOE_TAB)
        dst = pl.ds(pl.multiple_of(slot * MOE_TAB, MOE_TAB), MOE_TAB)
        return (pltpu.make_async_copy(tok_hbm.at[win], tok_s.at[dst], tsem.at[slot]),
                pltpu.make_async_copy(dst_hbm.at[win], dst_s.at[dst], tsem.at[slot]))

    def groups(m):
        return pl.ds(pl.multiple_of(m * gsub, gsub), gsub)

    def weight_copies(e, j, slot):
        c_in = pltpu.make_async_copy(win_hbm.at[e, :, pl.ds(pl.multiple_of(j * 2 * tf, 2 * tf), 2 * tf)],
                                     win_st.at[slot], wsem.at[slot])
        c_out = pltpu.make_async_copy(wout_hbm.at[e, pl.ds(pl.multiple_of(j * tf, tf), tf), :],
                                      wout_st.at[slot], wsem.at[slot])
        return c_in, c_out

    def gather_rows(m, toff):
        def body(i, carry):
            g = m * gsub + i
            for u in range(SUBLANES):
                tk = tok_s[toff + g * SUBLANES + u]
                pltpu.make_async_copy(h2p_hbm.at[pl.ds(tk, 1)], xbuf.at[g, pl.ds(u, 1)], gsem.at[m]).start()
            return carry
        lax.fori_loop(0, gsub, body, 0)

    def wait_gather(m):
        pltpu.make_async_copy(xbuf.at[groups(m)], xbuf.at[groups(m)], gsem.at[m]).wait()

    def scatter_rows(m, toff, nvalid):
        def body(i, carry):
            g = m * gsub + i
            for u in range(SUBLANES):
                d = dst_s[toff + g * SUBLANES + u]
                pltpu.make_async_copy(acc.at[g, pl.ds(u, 1)], y_hbm.at[pl.ds(d, 1)], ssem.at[m]).start()
            return carry
        n_full = nvalid // SUBLANES
        lax.fori_loop(0, n_full, body, 0)
        g_last = m * gsub + n_full

        def tail(u, carry):
            d = dst_s[toff + g_last * SUBLANES + u]
            pltpu.make_async_copy(acc.at[g_last, pl.ds(u, 1)], y_hbm.at[pl.ds(d, 1)], ssem.at[m]).start()
            return carry
        lax.fori_loop(0, nvalid - n_full * SUBLANES, tail, 0)

    def wait_scatter(m, nvalid):
        @pl.when(nvalid == sub)
        def _():
            pltpu.make_async_copy(acc.at[groups(m)], acc.at[groups(m)], ssem.at[m]).wait()

        @pl.when(nvalid < sub)
        def _():
            def body(i, carry):
                pltpu.make_async_copy(acc.at[0, pl.ds(0, 1)], y_hbm.at[pl.ds(0, 1)], ssem.at[m]).wait()
                return carry
            lax.fori_loop(0, nvalid, body, 0)

    def valid_rows(n_rows, m):
        return jnp.clip(n_rows - m * sub, 0, sub)

    xbuf[...] = jnp.zeros_like(xbuf)
    acc[...] = jnp.zeros_like(acc)
    for cp in table_copies(0, 0):
        cp.start()
    for cp in table_copies(0, 0):
        cp.wait()

    @pl.when(n_super > 0)
    def _():
        for cp in weight_copies(st_e[0], 0, 0):
            cp.start()
        toff0 = tab_off(0, 0)
        lax.fori_loop(0, n_sub_of(0), lambda m, c: (gather_rows(m, toff0), c)[1], 0)

    def supertile(s, carry):
        e = st_e[s]
        n_rows = st_n[s]
        n_sub = n_sub_of(s)
        tslot = s % 2
        toff = tab_off(s, tslot)
        n_rows_prev = jnp.where(s > 0, st_n[jnp.maximum(s - 1, 0)], 0)
        for cp in table_copies(s + 1, 1 - tslot):
            cp.start()
        n_sub_prev = (n_rows_prev + sub - 1) // sub
        lax.fori_loop(n_sub, jnp.maximum(n_sub, n_sub_prev),
                      lambda m, c: (wait_scatter(m, valid_rows(n_rows_prev, m)), c)[1], 0)

        def chunk(j, carry):
            slot = j % 2

            @pl.when(j < nf - 1)
            def _():
                for cp in weight_copies(e, j + 1, 1 - slot):
                    cp.start()

            @pl.when((j == nf - 1) & (s + 1 < n_super))
            def _():
                for cp in weight_copies(st_e[s + 1], 0, 1 - slot):
                    cp.start()

            for cp in weight_copies(e, j, slot):
                cp.wait()
            perm = perm_ref[...]
            rblk = 512
            for cgrp in range(2 * tf // 256):
                for rb in range(0, wi_bf.shape[0], rblk):
                    wblk = win_st[slot, rb:rb + rblk, cgrp * 256:(cgrp + 1) * 256].astype(BF16)
                    pw = jnp.dot(wblk, perm, preferred_element_type=F32).astype(BF16)
                    wi_bf[rb:rb + rblk, cgrp * LANES:(cgrp + 1) * LANES] = pw[:, :LANES]
                    wi_bf[rb:rb + rblk, tf + cgrp * LANES: tf + (cgrp + 1) * LANES] = pw[:, LANES:]
            wo_bf[...] = wout_st[slot].astype(BF16)
            b_in = bin_ref[e, pl.ds(j, 1), :]

            @pl.when(j == 0)
            def _():
                def ready(m, c2):
                    wait_gather(m)
                    wait_scatter(m, valid_rows(n_rows_prev, m))
                    return c2
                lax.fori_loop(0, n_sub, ready, 0)

            def stage1(m):
                x_lo, x_hi = _unpack_bf16_pairs(xbuf[groups(m)].reshape(sub, dh))
                hb = (jnp.dot(x_lo, wi_bf[0:dh, :], preferred_element_type=F32)
                      + jnp.dot(x_hi, wi_bf[dh:2 * dh, :], preferred_element_type=F32) + b_in)
                glu = jnp.minimum(hb[:, :tf], SWIGLU_LIMIT)
                lin = jnp.clip(hb[:, tf:], -SWIGLU_LIMIT, SWIGLU_LIMIT)
                act = glu * jax.nn.sigmoid(SWIGLU_ALPHA * glu) * (lin + 1.0)
                act_scr[m % 2] = act.astype(BF16)

            def stage2(m):
                o = jnp.dot(act_scr[m % 2], wo_bf[...], preferred_element_type=F32)
                prev = acc[groups(m)].reshape(sub, d_out)
                acc[groups(m)] = (jnp.where(j > 0, prev, 0.0) + o).reshape(gsub, SUBLANES, d_out)

            stage1(0)

            def subtile(m, carry):
                stage2(m)
                stage1(m + 1)
                return carry
            lax.fori_loop(0, n_sub - 1, subtile, 0)
            stage2(n_sub - 1)
            return carry
        lax.fori_loop(0, nf, chunk, 0)

        for cp in table_copies(s + 1, 1 - tslot):
            cp.wait()
        n_sub_next = n_sub_of(s + 1)
        toff_next = tab_off(s + 1, 1 - tslot)
        b_out = bout_ref[e]

        def finish(m, carry):
            acc[groups(m)] = acc[groups(m)] + b_out
            scatter_rows(m, toff, valid_rows(n_rows, m))

            @pl.when(m < n_sub_next)
            def _():
                gather_rows(m, toff_next)
            return carry
        lax.fori_loop(0, n_sub, finish, 0)
        lax.fori_loop(n_sub, jnp.maximum(n_sub, n_sub_next),
                      lambda m, c: (gather_rows(m, toff_next), c)[1], 0)
        return carry
    lax.fori_loop(0, n_super, supertile, 0)

    n_rows_last = jnp.where(n_super > 0, st_n[jnp.maximum(n_super - 1, 0)], 0)
    for m in range(nsub_max):
        wait_scatter(m, valid_rows(n_rows_last, m))


def _moe(h2p, row_tok, row_dst, st_e, st_n, st_start, n_super, w_in, w_out, b_in_c, b_out, perm, t):
    n_experts, d, f2 = w_in.shape
    nf = f2 // (2 * MOE_TF)
    rmax = MOE_ROWS_MAX
    nsub_max = rmax // MOE_SUB
    assert rmax % MOE_SUB == 0 and MOE_SUB % SUBLANES == 0
    assert MOE_TAB >= rmax + MOE_TAB_ALIGN and MOE_TAB % MOE_TAB_ALIGN == 0
    smem = pl.BlockSpec(memory_space=pltpu.SMEM)
    hbm = pl.BlockSpec(memory_space=pl.ANY)
    vmem = pl.BlockSpec(memory_space=pltpu.VMEM)
    kern = functools.partial(_moe_kernel, nf=nf)
    return pl.pallas_call(
        kern,
        in_specs=[smem, smem, smem, smem, hbm, hbm, hbm, hbm, hbm, vmem, vmem, vmem],
        out_specs=hbm,
        out_shape=jax.ShapeDtypeStruct((TOP_K * t, d), F32),
        scratch_shapes=[pltpu.SMEM((2 * MOE_TAB,), jnp.int32),
                        pltpu.SMEM((2 * MOE_TAB,), jnp.int32),
                        pltpu.VMEM((rmax // SUBLANES, SUBLANES, d // 2), jnp.uint32),
                        pltpu.VMEM((rmax // SUBLANES, SUBLANES, d), F32),
                        pltpu.VMEM((2, d, 2 * MOE_TF), F32),
                        pltpu.VMEM((2, MOE_TF, d), F32),
                        pltpu.VMEM((d, 2 * MOE_TF), BF16),
                        pltpu.VMEM((MOE_TF, d), BF16),
                        pltpu.VMEM((2, MOE_SUB, MOE_TF), BF16),
                        pltpu.SemaphoreType.DMA((2,)),
                        pltpu.SemaphoreType.DMA((nsub_max,)),
                        pltpu.SemaphoreType.DMA((nsub_max,)),
                        pltpu.SemaphoreType.DMA((2,))],
        compiler_params=pltpu.CompilerParams(vmem_limit_bytes=VMEM_LIMIT, has_side_effects=True),
        name="moe",
    )(st_e, st_n, st_start, n_super, row_tok, row_dst, h2p, w_in, w_out, b_in_c, b_out, perm)


def _combine_kernel(y0_ref, y1_ref, y2_ref, y3_ref, gate_ref, x1_ref, g2_ref, fg_ref, o_ref):
    g = gate_ref[...]
    y = (y0_ref[...] * g[:, 0:1] + y1_ref[...] * g[:, 1:2]) + (y2_ref[...] * g[:, 2:3] + y3_ref[...] * g[:, 3:4])
    x2 = x1_ref[...] + g2_ref[...] * y
    ms = jnp.mean(x2 * x2, axis=-1, keepdims=True)
    o_ref[...] = x2 * lax.rsqrt(ms + NORM_EPS) * fg_ref[...]


def _combine(y, gates, x1, g2, fg, tm):
    t, d = x1.shape
    nb = t // tm
    vec = pl.BlockSpec((1, d), lambda i: (0, 0))
    yspecs = [pl.BlockSpec((tm, d), functools.partial(lambda i, k: (k * nb + i, 0), k=k)) for k in range(TOP_K)]
    return pl.pallas_call(
        _combine_kernel,
        grid=(nb,),
        in_specs=yspecs + [pl.BlockSpec((tm, LANES), lambda i: (i, 0)),
                           pl.BlockSpec((tm, d), lambda i: (i, 0)), vec, vec],
        out_specs=pl.BlockSpec((tm, d), lambda i: (i, 0)),
        out_shape=jax.ShapeDtypeStruct((t, d), F32),
        compiler_params=_cparams(("arbitrary",)),
        name="combine",
    )(y, y, y, y, gates, x1, g2, fg)


def kernel(x, c, w_ada, b_ada, norm1_g, w_in_proj, conv_w, conv_b, dt_bias, a_log, d_skip, ssd_norm_g, w_pool,
           b_pool, pool_scale, w_out_proj, norm2_g, w_router, b_router, w_exp_in, b_exp_in, w_exp_out, b_exp_out,
           final_norm_g):
    assert x.shape[0] == 1 and w_ada.shape[0] == 1
    _, t, d = x.shape
    n_experts = w_router.shape[2]
    x1, h2p, idx, gates, g2 = _first_half(x, c, w_ada, b_ada, norm1_g, w_in_proj, conv_w, conv_b, dt_bias, a_log,
                                          d_skip, ssd_norm_g, w_pool, b_pool, pool_scale, w_out_proj, norm2_g,
                                          w_router, b_router)
    row_tok, row_dst, st_e, st_n, st_start, n_super = _route_tables(idx[:, :TOP_K], n_experts)
    f = w_exp_out.shape[2]
    nf = f // MOE_TF
    b_in = b_exp_in[0].reshape(n_experts, nf, MOE_TF, 2)
    b_in_c = jnp.concatenate([b_in[..., 0], b_in[..., 1]], axis=-1)
    src = jnp.concatenate([2 * jnp.arange(LANES), 2 * jnp.arange(LANES) + 1])
    perm = (jnp.arange(2 * LANES)[:, None] == src[None, :]).astype(BF16)
    y = _moe(h2p, row_tok, row_dst, st_e, st_n, st_start, n_super, w_exp_in.reshape(w_exp_in.shape[1:]),
             w_exp_out.reshape(w_exp_out.shape[1:]), b_in_c, b_exp_out.reshape(n_experts, 1, d), perm, t)
    out = _combine(y, gates, x1, g2, final_norm_g.reshape(1, d), min(256, t))
    return out.reshape(x.shape)
```

```python
import functools

import jax
import jax.numpy as jnp
from jax import lax
from jax.experimental import pallas as pl
from jax.experimental.pallas import tpu as pltpu

F32 = jnp.float32
BF16 = jnp.bfloat16
HIGHEST = lax.Precision.HIGHEST

SSD_HEAD_DIM = 64
SSD_GROUPS = 4
D_STATE = 128
CONV_K = 4
CHUNK = 128
POOL_WINDOWS = (2, 4, 8, 16)
TOP_K = 4
SWIGLU_LIMIT = 7.0
SWIGLU_ALPHA = 1.702
NORM_EPS = 1e-6

LANES = 128
SUBLANES = 8
VMEM_LIMIT = 56 * 1024 * 1024


def _cparams(sem, vmem=VMEM_LIMIT):
    return pltpu.CompilerParams(dimension_semantics=sem, vmem_limit_bytes=vmem)


def _silu(v):
    return v * jax.nn.sigmoid(v)


def _ada_kernel(c_ref, w_ref, b_ref, o_ref):
    cond = _silu(c_ref[...])
    cond8 = jnp.broadcast_to(cond, (SUBLANES, cond.shape[1]))
    o = jnp.dot(cond8, w_ref[...], preferred_element_type=F32, precision=HIGHEST)
    o_ref[...] = o[0:1] + b_ref[...]


def _ada(c, w, b):
    d, n = w.shape
    tn = 1536
    return pl.pallas_call(
        _ada_kernel,
        grid=(n // tn,),
        in_specs=[pl.BlockSpec((1, d), lambda j: (0, 0)),
                  pl.BlockSpec((d, tn), lambda j: (0, j)),
                  pl.BlockSpec((1, tn), lambda j: (0, j))],
        out_specs=pl.BlockSpec((1, tn), lambda j: (0, j)),
        out_shape=jax.ShapeDtypeStruct((1, n), F32),
        compiler_params=_cparams(("arbitrary",)),
        name="ada",
    )(c, w, b.reshape(1, n))


def _inproj_kernel(x_ref, g_ref, sc_ref, sh_ref, w_ref, wdt_ref, proj_ref, dt_ref, h_scr, *, n_heads):
    @pl.when(pl.program_id(1) == 0)
    def _():
        x = x_ref[...]
        ms = jnp.mean(x * x, axis=-1, keepdims=True)
        h = x * lax.rsqrt(ms + NORM_EPS) * g_ref[...]
        h = h * (1.0 + sc_ref[...]) + sh_ref[...]
        h_hi = h.astype(BF16)
        h_scr[...] = h_hi
        h_mid = (h - h_hi.astype(F32)).astype(BF16)
        wdt = wdt_ref[...]
        r_hi = jnp.dot(h_hi, wdt, preferred_element_type=F32)
        r_mid = jnp.dot(h_mid, wdt, preferred_element_type=F32)
        dt_ref[...] = (r_hi + r_mid) + pltpu.roll(r_hi, LANES - n_heads, axis=1)

    proj_ref[...] = jnp.dot(h_scr[...], w_ref[...], preferred_element_type=F32)


def _in_proj(x, g, sc, sh, w, wdt, tm, tn, n_heads):
    t, d = x.shape
    n = w.shape[1]
    vec = pl.BlockSpec((1, d), lambda i, j: (0, 0))
    return pl.pallas_call(
        functools.partial(_inproj_kernel, n_heads=n_heads),
        grid=(t // tm, n // tn),
        in_specs=[pl.BlockSpec((tm, d), lambda i, j: (i, 0)), vec, vec, vec,
                  pl.BlockSpec((d, tn), lambda i, j: (0, j)),
                  pl.BlockSpec((d, LANES), lambda i, j: (0, 0))],
        out_specs=[pl.BlockSpec((tm, tn), lambda i, j: (i, j)),
                   pl.BlockSpec((tm, LANES), lambda i, j: (i, 0))],
        out_shape=[jax.ShapeDtypeStruct((t, n), F32), jax.ShapeDtypeStruct((t, LANES), F32)],
        scratch_shapes=[pltpu.VMEM((tm, d), BF16)],
        compiler_params=_cparams(("arbitrary", "arbitrary")),
        name="in_proj",
    )(x, g, sc, sh, w, wdt)


def _ssd_pool_kernel(xbc_ref, xbch_ref, z0_ref, z1_ref, z2_ref, u_ref, uh_ref, dt_ref,
                     cw_ref, cb_ref, dtb_ref, alog_ref, dskip_ref, ng_ref, eh_ref,
                     wp_ref, bp_ref, ps_ref, y_ref,
                     st_scr, xe_scr, xa_scr, ue_scr, yd_scr, *, d_ssd, n_heads):
    i = pl.program_id(0)
    L = CHUNK
    gw = d_ssd // SSD_GROUPS
    halo_c = xe_scr.shape[0] - L
    halo_u = ue_scr.shape[0] - L
    first = i == 0

    @pl.when(first)
    def _():
        st_scr[...] = jnp.zeros_like(st_scr)

    xe_scr[0:halo_c, :] = jnp.where(first, 0.0, xbch_ref[...])
    xe_scr[halo_c:, :] = xbc_ref[...]
    ncols = xe_scr.shape[1]
    cblk = 512
    for c0 in range(0, ncols, cblk):
        acc = jnp.broadcast_to(cb_ref[:, c0:c0 + cblk], (L, cblk))
        for k in range(CONV_K):
            tap = xe_scr[pl.ds(halo_c - (CONV_K - 1) + k, L), c0:c0 + cblk]
            acc = acc + cw_ref[k:k + 1, c0:c0 + cblk] * tap
        xa_scr[:, c0:c0 + cblk] = _silu(acc)

    lane = lax.broadcasted_iota(jnp.int32, (L, LANES), 1)
    row = lax.broadcasted_iota(jnp.int32, (L, LANES), 0)
    head_ok = lane < n_heads
    dt = jnp.where(head_ok, jax.nn.softplus(dt_ref[...] + dtb_ref[...]), 0.0)
    a = -jnp.exp(alog_ref[...])
    da = dt * a
    causal = row >= lane
    tri = jnp.where(causal, 1.0, 0.0).astype(F32)
    cum = jnp.dot(tri, da, preferred_element_type=F32, precision=HIGHEST)
    cum_t = cum.T
    dt_t = dt.T
    cum_last = cum[L - 1:L, :]
    ecum = jnp.exp(cum)
    wend = jnp.exp(cum_last - cum) * dt
    eh = eh_ref[...]
    ecum_x = jnp.dot(ecum.astype(BF16), eh, preferred_element_type=F32)
    wend_x = jnp.dot(wend.astype(BF16), eh, preferred_element_type=F32)

    hpg = n_heads // SSD_GROUPS
    for g in range(SSD_GROUPS):
        b_g = xa_scr[:, d_ssd + g * D_STATE: d_ssd + (g + 1) * D_STATE]
        c_g = xa_scr[:, d_ssd + SSD_GROUPS * D_STATE + g * D_STATE:
                     d_ssd + SSD_GROUPS * D_STATE + (g + 1) * D_STATE]
        b_bf = b_g.astype(BF16)
        c_bf = c_g.astype(BF16)
        cb = lax.dot_general(c_bf, b_bf, (((1,), (1,)), ((), ())), preferred_element_type=F32)
        for hp in range(hpg // 2):
            h0 = g * hpg + 2 * hp
            ms = []
            for h in (h0, h0 + 1):
                seg = cum[:, h:h + 1] - cum_t[h:h + 1, :]
                decay = jnp.exp(jnp.where(causal, seg, -jnp.inf))
                ms.append(cb * decay * dt_t[h:h + 1, :])
            lhs = jnp.concatenate(ms, axis=1).astype(BF16)
            c0 = h0 * SSD_HEAD_DIM
            xpair = xa_scr[:, c0:c0 + LANES]
            top = jnp.where(lane < SSD_HEAD_DIM, xpair, 0.0)
            bot = jnp.where(lane >= SSD_HEAD_DIM, xpair, 0.0)
            rhs = jnp.concatenate([top, bot], axis=0).astype(BF16)
            yd_scr[:, c0:c0 + LANES] = jnp.dot(lhs, rhs, preferred_element_type=F32)
        gs = slice(g * gw, (g + 1) * gw)
        st = st_scr[g]
        y_off = jnp.dot(c_bf, st.astype(BF16), preferred_element_type=F32) * ecum_x[:, gs]
        xs_g = xa_scr[:, gs]
        y_g = yd_scr[:, gs] + y_off + dskip_ref[:, gs] * xs_g
        xw = (xs_g * wend_x[:, gs]).astype(BF16)
        st_scr[g] = st * ecum_x[L - 1:L, gs] + jnp.dot(b_g.T.astype(BF16), xw, preferred_element_type=F32)
        zparts = (z0_ref, z1_ref, z2_ref)
        zw = z0_ref.shape[1]
        zg = jnp.concatenate(
            [zparts[(g * gw + o) // zw][:, (g * gw + o) % zw:(g * gw + o) % zw + LANES] for o in range(0, gw, LANES)],
            axis=1)
        y_g = y_g * _silu(zg)
        msq = jnp.mean(y_g * y_g, axis=-1, keepdims=True)
        y_ref[:, gs] = (y_g * lax.rsqrt(msq + NORM_EPS) * ng_ref[:, gs]).astype(y_ref.dtype)

    ue_scr[0:halo_u, :] = jnp.where(first, 0.0, uh_ref[...])
    ue_scr[halo_u:, :] = u_ref[...]
    pg = ue_scr.shape[1] // len(POOL_WINDOWS)
    tpos = (i * L + lax.broadcasted_iota(jnp.int32, (L, pg), 0) + 1).astype(F32)
    for gi, w in enumerate(POOL_WINDOWS):
        cs = slice(gi * pg, (gi + 1) * pg)
        tok = ue_scr[halo_u:, cs]
        win = tok
        for d in range(1, w):
            win = win + ue_scr[pl.ds(halo_u - d, L), cs]
        pooled = win / jnp.minimum(tpos, float(w)) - tok
        yp = jnp.dot(pooled.astype(BF16), wp_ref[gi], preferred_element_type=F32)
        y_ref[:, d_ssd + gi * pg: d_ssd + (gi + 1) * pg] = ((yp + bp_ref[:, cs]) * ps_ref[:, cs]).astype(y_ref.dtype)


def _ssd_pool(proj, dt, cw, cb, dtb, alog, dskip_x, ng, eh, wp, bp, ps, d_ssd, d_conv, d_pool, n_heads):
    t = proj.shape[0]
    L = CHUNK
    halo_c, halo_u = 8, 16
    zw = 1024
    zb = d_conv // zw
    ub = (d_conv + d_ssd) // d_pool
    d_mix = d_ssd + d_pool
    gw = d_ssd // SSD_GROUPS

    def full(shape):
        return pl.BlockSpec(shape, lambda i: (0,) * len(shape))

    in_specs = [
        pl.BlockSpec((L, d_conv), lambda i: (i, 0)),
        pl.BlockSpec((halo_c, d_conv), lambda i: (jnp.maximum(i * (L // halo_c) - 1, 0), 0)),
        pl.BlockSpec((L, zw), lambda i: (i, zb)),
        pl.BlockSpec((L, zw), lambda i: (i, zb + 1)),
        pl.BlockSpec((L, zw), lambda i: (i, zb + 2)),
        pl.BlockSpec((L, d_pool), lambda i: (i, ub)),
        pl.BlockSpec((halo_u, d_pool), lambda i: (jnp.maximum(i * (L // halo_u) - 1, 0), ub)),
        pl.BlockSpec((L, LANES), lambda i: (i, 0)),
        full(cw.shape), full(cb.shape), full(dtb.shape), full(alog.shape), full(dskip_x.shape),
        full(ng.shape), full(eh.shape), full(wp.shape), full(bp.shape), full(ps.shape),
    ]
    kern = functools.partial(_ssd_pool_kernel, d_ssd=d_ssd, n_heads=n_heads)
    return pl.pallas_call(
        kern,
        grid=(t // L,),
        in_specs=in_specs,
        out_specs=pl.BlockSpec((L, d_mix), lambda i: (i, 0)),
        out_shape=jax.ShapeDtypeStruct((t, d_mix), BF16),
        scratch_shapes=[pltpu.VMEM((SSD_GROUPS, D_STATE, gw), F32),
                        pltpu.VMEM((L + halo_c, d_conv), F32),
                        pltpu.VMEM((L, d_conv), F32),
                        pltpu.VMEM((L + halo_u, d_pool), F32),
                        pltpu.VMEM((L, d_ssd), F32)],
        compiler_params=_cparams(("arbitrary",)),
        name="ssd_pool",
    )(proj, proj, proj, proj, proj, proj, proj, dt, cw, cb, dtb, alog, dskip_x, ng, eh, wp, bp, ps)


def _pack_bf16_pairs(lo, hi):
    lo_b = pltpu.bitcast(lo.astype(BF16).astype(F32), jnp.uint32) >> 16
    hi_b = pltpu.bitcast(hi.astype(BF16).astype(F32), jnp.uint32) & jnp.uint32(0xFFFF0000)
    return lo_b | hi_b


def _unpack_bf16_pairs(w):
    lo = pltpu.bitcast(w << 16, F32).astype(BF16)
    hi = pltpu.bitcast(w & jnp.uint32(0xFFFF0000), F32).astype(BF16)
    return lo, hi


def _outproj_kernel(y_ref, x_ref, w_ref, g1_ref, ng_ref, sc_ref, sh_ref, wr_ref, br_ref,
                    x1_ref, h2p_ref, idx_ref, gate_ref, *, n_experts):
    mix = jnp.dot(y_ref[...], w_ref[...], preferred_element_type=F32)
    x1 = x_ref[...] + g1_ref[...] * mix
    x1_ref[...] = x1
    ms = jnp.mean(x1 * x1, axis=-1, keepdims=True)
    h = x1 * lax.rsqrt(ms + NORM_EPS) * ng_ref[...]
    h = h * (1.0 + sc_ref[...]) + sh_ref[...]
    half = h.shape[1] // 2
    h2p_ref[...] = _pack_bf16_pairs(h[:, :half], h[:, half:])

    h_hi = h.astype(BF16)
    h_mid = (h - h_hi.astype(F32)).astype(BF16)
    wr = wr_ref[...]
    r_hi = jnp.dot(h_hi, wr, preferred_element_type=F32)
    r_mid = jnp.dot(h_mid, wr, preferred_element_type=F32)
    logits = (r_hi + r_mid) + pltpu.roll(r_hi, LANES - n_experts, axis=1) + br_ref[...]
    tm = logits.shape[0]
    lane = lax.broadcasted_iota(jnp.int32, (tm, LANES), 1)
    lane_f = lane.astype(F32)
    vals = jnp.where(lane < n_experts, logits, -jnp.inf)
    top_v, top_i = [], []
    for _ in range(TOP_K):
        m = jnp.max(vals, axis=-1, keepdims=True)
        am = jnp.min(jnp.where(vals == m, lane_f, float(LANES)), axis=-1, keepdims=True)
        top_v.append(m)
        top_i.append(am)
        vals = jnp.where(lane_f == am, -jnp.inf, vals)
    es = [jnp.exp(v - top_v[0]) for v in top_v]
    denom = es[0] + es[1] + es[2] + es[3]
    idx_out = jnp.zeros((tm, LANES), F32)
    gate_out = jnp.zeros((tm, LANES), F32)
    for k in range(TOP_K):
        idx_out = jnp.where(lane == k, top_i[k], idx_out)
        gate_out = jnp.where(lane == k, es[k] / denom, gate_out)
    idx_ref[...] = idx_out.astype(jnp.int32)
    gate_ref[...] = gate_out


def _out_proj(ycat, x, w, g1, ng, sc, sh, wr, br, n_experts, tm):
    t, d = x.shape
    dm = ycat.shape[1]
    vec = pl.BlockSpec((1, d), lambda i: (0, 0))
    kern = functools.partial(_outproj_kernel, n_experts=n_experts)
    return pl.pallas_call(
        kern,
        grid=(t // tm,),
        in_specs=[pl.BlockSpec((tm, dm), lambda i: (i, 0)),
                  pl.BlockSpec((tm, d), lambda i: (i, 0)),
                  pl.BlockSpec((dm, d), lambda i: (0, 0), pipeline_mode=pl.Buffered(1)),
                  vec, vec, vec, vec,
                  pl.BlockSpec((d, LANES), lambda i: (0, 0)),
                  pl.BlockSpec((1, LANES), lambda i: (0, 0))],
        out_specs=[pl.BlockSpec((tm, d), lambda i: (i, 0)),
                   pl.BlockSpec((tm, d // 2), lambda i: (i, 0)),
                   pl.BlockSpec((tm, LANES), lambda i: (i, 0)),
                   pl.BlockSpec((tm, LANES), lambda i: (i, 0))],
        out_shape=[jax.ShapeDtypeStruct((t, d), F32),
                   jax.ShapeDtypeStruct((t, d // 2), jnp.uint32),
                   jax.ShapeDtypeStruct((t, LANES), jnp.int32),
                   jax.ShapeDtypeStruct((t, LANES), F32)],
        compiler_params=_cparams(("arbitrary",)),
        name="out_proj",
    )(ycat, x, w, g1, ng, sc, sh, wr, br)


def _first_half(x, c, w_ada, b_ada, norm1_g, w_in_proj, conv_w, conv_b, dt_bias, a_log, d_skip, ssd_norm_g,
                w_pool, b_pool, pool_scale, w_out_proj, norm2_g, w_router, b_router):
    _, t, d = x.shape
    n_heads = dt_bias.shape[1]
    d_ssd = n_heads * SSD_HEAD_DIM
    d_conv = conv_w.shape[2]
    d_pool = b_pool.shape[1]
    n_experts = w_router.shape[2]
    x2 = x.reshape(t, d)

    mod = _ada(c, w_ada[0], b_ada[0])
    sh1, sc1, g1, sh2, sc2, g2 = [mod[:, k * d:(k + 1) * d] for k in range(6)]

    wi = w_in_proj[0]
    w_main = jnp.concatenate([wi[:, d_ssd:d_ssd + d_conv], wi[:, :d_ssd], wi[:, d_ssd + d_conv + n_heads:]],
                             axis=1).astype(BF16)
    w_dt32 = wi[:, d_ssd + d_conv:d_ssd + d_conv + n_heads]
    w_dt_hi = w_dt32.astype(BF16)
    w_dt_mid = (w_dt32 - w_dt_hi.astype(F32)).astype(BF16)
    assert 2 * n_heads <= LANES
    w_dt = jnp.pad(jnp.concatenate([w_dt_hi, w_dt_mid], axis=1), ((0, 0), (0, LANES - 2 * n_heads)))
    tm1 = min(1024, t)
    proj, dt_raw = _in_proj(x2, norm1_g, sc1, sh1, w_main, w_dt, tm1, 1024, n_heads)

    pad_h = LANES - n_heads
    dtb = jnp.pad(dt_bias, ((0, 0), (0, pad_h)))
    alog = jnp.pad(a_log, ((0, 0), (0, pad_h)))
    dskip_x = jnp.repeat(d_skip, SSD_HEAD_DIM, axis=1)
    eh = (jnp.arange(LANES)[:, None] == (jnp.arange(d_ssd) // SSD_HEAD_DIM)[None, :]).astype(BF16)
    ycat = _ssd_pool(proj, dt_raw, conv_w[0], conv_b, dtb, alog, dskip_x, ssd_norm_g, eh,
                     w_pool[0].astype(BF16), b_pool, pool_scale, d_ssd, d_conv, d_pool, n_heads)

    wr_hi = w_router[0].astype(BF16)
    wr_mid = (w_router[0] - wr_hi.astype(F32)).astype(BF16)
    assert 2 * n_experts <= LANES
    wr = jnp.pad(jnp.concatenate([wr_hi, wr_mid], axis=1), ((0, 0), (0, LANES - 2 * n_experts)))
    br = jnp.pad(b_router, ((0, 0), (0, LANES - n_experts)))
    x1, h2p, idx, gates = _out_proj(ycat, x2, w_out_proj[0].astype(BF16), g1, norm2_g, sc2, sh2, wr, br,
                                    n_experts, min(512, t))
    return x1, h2p, idx, gates, g2


MOE_SUB = 256
MOE_ROWS_MAX = 2304
MOE_TF = 256
MOE_TAB_ALIGN = 1024
MOE_TAB = 4096


def _route_tables(idx, n_experts):
    t, k = idx.shape
    rmax, sub = MOE_ROWS_MAX, MOE_SUB
    ns_max = n_experts + (t * k + rmax - 1) // rmax
    tok = jnp.arange(t, dtype=jnp.int32)[:, None]
    key = (idx * t + tok) * k + jnp.arange(k, dtype=jnp.int32)[None, :]
    skey = jnp.sort(key.reshape(-1))
    row_tok = (skey // k) % t
    row_dst = (skey % k) * t + row_tok
    bounds = jnp.searchsorted(skey, jnp.arange(n_experts + 1, dtype=jnp.int32) * (t * k)).astype(jnp.int32)
    off, counts = bounds[:-1], bounds[1:] - bounds[:-1]
    n_sup = (counts + rmax - 1) // rmax
    per = (counts + jnp.maximum(n_sup, 1) - 1) // jnp.maximum(n_sup, 1)
    rps = jnp.maximum((per + sub - 1) // sub * sub, sub)
    sup_end = jnp.cumsum(n_sup)
    sup_base = sup_end - n_sup
    s_ids = jnp.arange(ns_max + 1, dtype=jnp.int32)
    live = s_ids < sup_end[-1]
    st_e = jnp.minimum(jnp.searchsorted(sup_end, s_ids, side="right"), n_experts - 1).astype(jnp.int32)
    s_loc = s_ids - sup_base[st_e]
    st_n = jnp.where(live, jnp.clip(counts[st_e] - s_loc * rps[st_e], 0, rps[st_e]), 0).astype(jnp.int32)
    st_start = jnp.where(live, off[st_e] + s_loc * rps[st_e], 0).astype(jnp.int32)
    n_super = sup_end[-1:].astype(jnp.int32)
    row_tok = jnp.pad(row_tok, (0, MOE_TAB))
    row_dst = jnp.pad(row_dst, (0, MOE_TAB))
    return row_tok, row_dst, st_e, st_n, st_start, n_super


def _moe_kernel(st_e, st_n, st_start, nsup, tok_hbm, dst_hbm, h2p_hbm, win_hbm, wout_hbm, bin_ref, bout_ref,
                perm_ref, y_hbm, tok_s, dst_s, xbuf, acc, ybuf, win_st, wout_st, wi_bf, wo_bf, act_scr,
                tsem, gsem, ssem, wsem, *, nf):
    sub, rmax, tf = MOE_SUB, MOE_ROWS_MAX, MOE_TF
    nsub_max = rmax // sub
    gsub = sub // SUBLANES
    dh = xbuf.shape[2]
    d_out = acc.shape[2]
    n_super = nsup[0]

    def n_sub_of(s):
        return (st_n[s] + sub - 1) // sub

    def tab_base(s):
        return pl.multiple_of((st_start[s] // MOE_TAB_ALIGN) * MOE_TAB_ALIGN, MOE_TAB_ALIGN)

    def tab_off(s, slot):
        return slot * MOE_TAB + st_start[s] - tab_base(s)

    def table_copies(s, slot):
        win = pl.ds(tab_base(s), MOE_TAB)
        dst = pl.ds(pl.multiple_of(slot * MOE_TAB, MOE_TAB), MOE_TAB)
        return (pltpu.make_async_copy(tok_hbm.at[win], tok_s.at[dst], tsem.at[slot]),
                pltpu.make_async_copy(dst_hbm.at[win], dst_s.at[dst], tsem.at[slot]))

    def groups(m):
        return pl.ds(pl.multiple_of(m * gsub, gsub), gsub)

    def weight_copies(e, j):
        c_in = pltpu.make_async_copy(win_hbm.at[e, :, pl.ds(pl.multiple_of(j * 2 * tf, 2 * tf), 2 * tf)],
                                     win_st, wsem.at[0])
        c_out = pltpu.make_async_copy(wout_hbm.at[e, pl.ds(pl.multiple_of(j * tf, tf), tf), :],
                                      wout_st, wsem.at[0])
        return c_in, c_out

    def gather_rows(m, toff):
        def body(i, carry):
            g = m * gsub + i
            for u in range(SUBLANES):
                tk = tok_s[toff + g * SUBLANES + u]
                pltpu.make_async_copy(h2p_hbm.at[pl.ds(tk, 1)], xbuf.at[g, pl.ds(u, 1)], gsem.at[m]).start()
            return carry
        lax.fori_loop(0, gsub, body, 0)

    def wait_gather(m):
        pltpu.make_async_copy(xbuf.at[groups(m)], xbuf.at[groups(m)], gsem.at[m]).wait()

    def scatter_rows(m, toff, nvalid):
        def body(i, carry):
            g = m * gsub + i
            for u in range(SUBLANES):
                d = dst_s[toff + g * SUBLANES + u]
                pltpu.make_async_copy(ybuf.at[g, pl.ds(u, 1)], y_hbm.at[pl.ds(d, 1)], ssem.at[m]).start()
            return carry
        n_full = nvalid // SUBLANES
        lax.fori_loop(0, n_full, body, 0)
        g_last = m * gsub + n_full

        def tail(u, carry):
            d = dst_s[toff + g_last * SUBLANES + u]
            pltpu.make_async_copy(ybuf.at[g_last, pl.ds(u, 1)], y_hbm.at[pl.ds(d, 1)], ssem.at[m]).start()
            return carry
        lax.fori_loop(0, nvalid - n_full * SUBLANES, tail, 0)

    def wait_scatter(m, nvalid):
        @pl.when(nvalid == sub)
        def _():
            pltpu.make_async_copy(ybuf.at[groups(m)], ybuf.at[groups(m)], ssem.at[m]).wait()

        @pl.when(nvalid < sub)
        def _():
            def body(i, carry):
                pltpu.make_async_copy(ybuf.at[0, pl.ds(0, 1)], y_hbm.at[pl.ds(0, 1)], ssem.at[m]).wait()
                return carry
            lax.fori_loop(0, nvalid, body, 0)

    def valid_rows(n_rows, m):
        return jnp.clip(n_rows - m * sub, 0, sub)

    xbuf[...] = jnp.zeros_like(xbuf)
    acc[...] = jnp.zeros_like(acc)
    for cp in table_copies(0, 0):
        cp.start()
    for cp in table_copies(0, 0):
        cp.wait()

    @pl.when(n_super > 0)
    def _():
        for cp in weight_copies(st_e[0], 0):
            cp.start()
        toff0 = tab_off(0, 0)
        lax.fori_loop(0, n_sub_of(0), lambda m, c: (gather_rows(m, toff0), c)[1], 0)

    def supertile(s, carry):
        e = st_e[s]
        n_rows = st_n[s]
        n_sub = n_sub_of(s)
        tslot = s % 2
        toff = tab_off(s, tslot)
        n_rows_prev = jnp.where(s > 0, st_n[jnp.maximum(s - 1, 0)], 0)
        for cp in table_copies(s + 1, 1 - tslot):
            cp.start()

        def chunk(j, carry):
            for cp in weight_copies(e, j):
                cp.wait()
            perm = perm_ref[...]
            rblk = 512
            for cgrp in range(2 * tf // 256):
                for rb in range(0, wi_bf.shape[0], rblk):
                    wblk = win_st[rb:rb + rblk, cgrp * 256:(cgrp + 1) * 256].astype(BF16)
                    pw = jnp.dot(wblk, perm, preferred_element_type=F32).astype(BF16)
                    wi_bf[rb:rb + rblk, cgrp * LANES:(cgrp + 1) * LANES] = pw[:, :LANES]
                    wi_bf[rb:rb + rblk, tf + cgrp * LANES: tf + (cgrp + 1) * LANES] = pw[:, LANES:]
            wo_bf[...] = wout_st[...].astype(BF16)
            b_in = bin_ref[e, pl.ds(j, 1), :]

            @pl.when(j < nf - 1)
            def _():
                for cp in weight_copies(e, j + 1):
                    cp.start()

            @pl.when((j == nf - 1) & (s + 1 < n_super))
            def _():
                for cp in weight_copies(st_e[s + 1], 0):
                    cp.start()

            @pl.when(j == 0)
            def _():
                lax.fori_loop(0, n_sub, lambda m, c2: (wait_gather(m), c2)[1], 0)

            act_a, act_b = act_scr.at[0], act_scr.at[1]

            def stage1(m, act_ref):
                x_lo, x_hi = _unpack_bf16_pairs(xbuf[groups(m)].reshape(sub, dh))
                hb = (jnp.dot(x_lo, wi_bf[0:dh, :], preferred_element_type=F32)
                      + jnp.dot(x_hi, wi_bf[dh:2 * dh, :], preferred_element_type=F32) + b_in)
                glu = jnp.minimum(hb[:, :tf], SWIGLU_LIMIT)
                lin = jnp.clip(hb[:, tf:], -SWIGLU_LIMIT, SWIGLU_LIMIT)
                act = glu * jax.nn.sigmoid(SWIGLU_ALPHA * glu) * (lin + 1.0)
                act_ref[...] = act.astype(BF16)

            def stage2(m, act_ref):
                o = jnp.dot(act_ref[...], wo_bf[...], preferred_element_type=F32)
                prev = acc[groups(m)].reshape(sub, d_out)
                acc[groups(m)] = (jnp.where(j > 0, prev, 0.0) + o).reshape(gsub, SUBLANES, d_out)

            stage1(0, act_a)
            n_pairs = (n_sub - 1) // 2

            def pair(p, carry):
                m = 2 * p
                stage1(m + 1, act_b)
                stage2(m, act_a)
                stage1(m + 2, act_a)
                stage2(m + 1, act_b)
                return carry
            lax.fori_loop(0, n_pairs, pair, 0)
            m_last = 2 * n_pairs

            @pl.when(m_last < n_sub - 1)
            def _():
                stage1(m_last + 1, act_b)
                stage2(m_last, act_a)
                stage2(m_last + 1, act_b)

            @pl.when(m_last == n_sub - 1)
            def _():
                stage2(m_last, act_a)
            return carry
        lax.fori_loop(0, nf, chunk, 0)

        for cp in table_copies(s + 1, 1 - tslot):
            cp.wait()
        toff_next = tab_off(s + 1, 1 - tslot)
        lax.fori_loop(0, n_sub_of(s + 1), lambda m, c: (gather_rows(m, toff_next), c)[1], 0)
        n_sub_prev = (n_rows_prev + sub - 1) // sub
        lax.fori_loop(n_sub, jnp.maximum(n_sub, n_sub_prev),
                      lambda m, c: (wait_scatter(m, valid_rows(n_rows_prev, m)), c)[1], 0)
        b_out = bout_ref[e]

        def finish(m, carry):
            wait_scatter(m, valid_rows(n_rows_prev, m))
            v = acc[groups(m)].reshape(sub, d_out) + b_out
            ybuf[groups(m)] = _pack_bf16_pairs(v[:, :dh], v[:, dh:]).reshape(gsub, SUBLANES, dh)
            scatter_rows(m, toff, valid_rows(n_rows, m))
            return carry
        lax.fori_loop(0, n_sub, finish, 0)
        return carry
    lax.fori_loop(0, n_super, supertile, 0)

    n_rows_last = jnp.where(n_super > 0, st_n[jnp.maximum(n_super - 1, 0)], 0)
    for m in range(nsub_max):
        wait_scatter(m, valid_rows(n_rows_last, m))


def _moe(h2p, row_tok, row_dst, st_e, st_n, st_start, n_super, w_in, w_out, b_in_c, b_out, perm, t):
    n_experts, d, f2 = w_in.shape
    nf = f2 // (2 * MOE_TF)
    rmax = MOE_ROWS_MAX
    nsub_max = rmax // MOE_SUB
    assert rmax % MOE_SUB == 0 and MOE_SUB % SUBLANES == 0
    assert MOE_TAB >= rmax + MOE_TAB_ALIGN and MOE_TAB % MOE_TAB_ALIGN == 0
    smem = pl.BlockSpec(memory_space=pltpu.SMEM)
    hbm = pl.BlockSpec(memory_space=pl.ANY)
    vmem = pl.BlockSpec(memory_space=pltpu.VMEM)
    kern = functools.partial(_moe_kernel, nf=nf)
    return pl.pallas_call(
        kern,
        in_specs=[smem, smem, smem, smem, hbm, hbm, hbm, hbm, hbm, vmem, vmem, vmem],
        out_specs=hbm,
        out_shape=jax.ShapeDtypeStruct((TOP_K * t, d // 2), jnp.uint32),
        scratch_shapes=[pltpu.SMEM((2 * MOE_TAB,), jnp.int32),
                        pltpu.SMEM((2 * MOE_TAB,), jnp.int32),
                        pltpu.VMEM((rmax // SUBLANES, SUBLANES, d // 2), jnp.uint32),
                        pltpu.VMEM((rmax // SUBLANES, SUBLANES, d), F32),
                        pltpu.VMEM((rmax // SUBLANES, SUBLANES, d // 2), jnp.uint32),
                        pltpu.VMEM((d, 2 * MOE_TF), F32),
                        pltpu.VMEM((MOE_TF, d), F32),
                        pltpu.VMEM((d, 2 * MOE_TF), BF16),
                        pltpu.VMEM((MOE_TF, d), BF16),
                        pltpu.VMEM((2, MOE_SUB, MOE_TF), BF16),
                        pltpu.SemaphoreType.DMA((2,)),
                        pltpu.SemaphoreType.DMA((nsub_max,)),
                        pltpu.SemaphoreType.DMA((nsub_max,)),
                        pltpu.SemaphoreType.DMA((1,))],
        compiler_params=pltpu.CompilerParams(vmem_limit_bytes=VMEM_LIMIT, has_side_effects=True),
        name="moe",
    )(st_e, st_n, st_start, n_super, row_tok, row_dst, h2p, w_in, w_out, b_in_c, b_out, perm)


def _combine_kernel(y0_ref, y1_ref, y2_ref, y3_ref, gate_ref, x1_ref, g2_ref, fg_ref, o_ref):
    g = gate_ref[...]
    half = y0_ref.shape[1]
    y_lo = jnp.zeros(y0_ref.shape, F32)
    y_hi = jnp.zeros(y0_ref.shape, F32)
    for k, y_ref in enumerate((y0_ref, y1_ref, y2_ref, y3_ref)):
        w = y_ref[...]
        y_lo = y_lo + pltpu.bitcast(w << 16, F32) * g[:, k:k + 1]
        y_hi = y_hi + pltpu.bitcast(w & jnp.uint32(0xFFFF0000), F32) * g[:, k:k + 1]
    x_lo = x1_ref[:, :half] + g2_ref[:, :half] * y_lo
    x_hi = x1_ref[:, half:] + g2_ref[:, half:] * y_hi
    ssq = jnp.sum(x_lo * x_lo, axis=-1, keepdims=True) + jnp.sum(x_hi * x_hi, axis=-1, keepdims=True)
    inv = lax.rsqrt(ssq / (2 * half) + NORM_EPS)
    o_ref[:, :half] = x_lo * inv * fg_ref[:, :half]
    o_ref[:, half:] = x_hi * inv * fg_ref[:, half:]


def _combine(y, gates, x1, g2, fg, tm):
    t, d = x1.shape
    nb = t // tm
    vec = pl.BlockSpec((1, d), lambda i: (0, 0))
    yspecs = [pl.BlockSpec((tm, d // 2), functools.partial(lambda i, k: (k * nb + i, 0), k=k))
              for k in range(TOP_K)]
    return pl.pallas_call(
        _combine_kernel,
        grid=(nb,),
        in_specs=yspecs + [pl.BlockSpec((tm, LANES), lambda i: (i, 0)),
                           pl.BlockSpec((tm, d), lambda i: (i, 0)), vec, vec],
        out_specs=pl.BlockSpec((tm, d), lambda i: (i, 0)),
        out_shape=jax.ShapeDtypeStruct((t, d), F32),
        compiler_params=_cparams(("arbitrary",)),
        name="combine",
    )(y, y, y, y, gates, x1, g2, fg)


def kernel(x, c, w_ada, b_ada, norm1_g, w_in_proj, conv_w, conv_b, dt_bias, a_log, d_skip, ssd_norm_g, w_pool,
           b_pool, pool_scale, w_out_proj, norm2_g, w_router, b_router, w_exp_in, b_exp_in, w_exp_out, b_exp_out,
           final_norm_g):
    assert x.shape[0] == 1 and w_ada.shape[0] == 1
    _, t, d = x.shape
    n_experts = w_router.shape[2]
    x1, h2p, idx, gates, g2 = _first_half(x, c, w_ada, b_ada, norm1_g, w_in_proj, conv_w, conv_b, dt_bias, a_log,
                                          d_skip, ssd_norm_g, w_pool, b_pool, pool_scale, w_out_proj, norm2_g,
                                          w_router, b_router)
    row_tok, row_dst, st_e, st_n, st_start, n_super = _route_tables(idx[:, :TOP_K], n_experts)
    f = w_exp_out.shape[2]
    nf = f // MOE_TF
    b_in = b_exp_in[0].reshape(n_experts, nf, MOE_TF, 2)
    b_in_c = jnp.concatenate([b_in[..., 0], b_in[..., 1]], axis=-1)
    src = jnp.concatenate([2 * jnp.arange(LANES), 2 * jnp.arange(LANES) + 1])
    perm = (jnp.arange(2 * LANES)[:, None] == src[None, :]).astype(BF16)
    y = _moe(h2p, row_tok, row_dst, st_e, st_n, st_start, n_super, w_exp_in.reshape(w_exp_in.shape[1:]),
             w_exp_out.reshape(w_exp_out.shape[1:]), b_in_c, b_exp_out.reshape(n_experts, 1, d), perm, t)
    out = _combine(y, gates, x1, g2, final_norm_g.reshape(1, d), min(256, t))
    return out.reshape(x.shape)
```

```python
import functools

import jax
import jax.numpy as jnp
from jax import lax
from jax.experimental import pallas as pl
from jax.experimental.pallas import tpu as pltpu

F32 = jnp.float32
BF16 = jnp.bfloat16
HIGHEST = lax.Precision.HIGHEST

SSD_HEAD_DIM = 64
SSD_GROUPS = 4
D_STATE = 128
CONV_K = 4
CHUNK = 128
POOL_WINDOWS = (2, 4, 8, 16)
TOP_K = 4
SWIGLU_LIMIT = 7.0
SWIGLU_ALPHA = 1.702
NORM_EPS = 1e-6

LANES = 128
SUBLANES = 8
VMEM_LIMIT = 56 * 1024 * 1024


def _cparams(sem, vmem=VMEM_LIMIT):
    return pltpu.CompilerParams(dimension_semantics=sem, vmem_limit_bytes=vmem)


def _silu(v):
    return v * jax.nn.sigmoid(v)


def _ada_kernel(c_ref, w_ref, b_ref, o_ref):
    cond = _silu(c_ref[...])
    cond8 = jnp.broadcast_to(cond, (SUBLANES, cond.shape[1]))
    o = jnp.dot(cond8, w_ref[...], preferred_element_type=F32, precision=HIGHEST)
    o_ref[...] = o[0:1] + b_ref[...]


def _ada(c, w, b):
    _, d, n = w.shape
    tn = 1536
    return pl.pallas_call(
        _ada_kernel,
        grid=(n // tn,),
        in_specs=[pl.BlockSpec((1, d), lambda j: (0, 0)),
                  pl.BlockSpec((None, d, tn), lambda j: (0, 0, j)),
                  pl.BlockSpec((1, tn), lambda j: (0, j))],
        out_specs=pl.BlockSpec((1, tn), lambda j: (0, j)),
        out_shape=jax.ShapeDtypeStruct((1, n), F32),
        compiler_params=_cparams(("arbitrary",)),
        name="ada",
    )(c, w, b)


def _inproj_kernel(x_ref, g_ref, sc_ref, sh_ref, w_ref, wdt_ref, proj_ref, dt_ref, h_scr, *, n_heads):
    @pl.when(pl.program_id(1) == 0)
    def _():
        x = x_ref[...]
        ms = jnp.mean(x * x, axis=-1, keepdims=True)
        h = x * lax.rsqrt(ms + NORM_EPS) * g_ref[...]
        h = h * (1.0 + sc_ref[...]) + sh_ref[...]
        h_hi = h.astype(BF16)
        h_scr[...] = h_hi
        h_mid = (h - h_hi.astype(F32)).astype(BF16)
        wdt = wdt_ref[...]
        r_hi = jnp.dot(h_hi, wdt, preferred_element_type=F32)
        r_mid = jnp.dot(h_mid, wdt, preferred_element_type=F32)
        dt_ref[...] = (r_hi + r_mid) + pltpu.roll(r_hi, LANES - n_heads, axis=1)

    proj_ref[...] = jnp.dot(h_scr[...], w_ref[...], preferred_element_type=F32)


def _in_proj(x, g, sc, sh, w, wdt, tm, tn, n_heads):
    t, d = x.shape
    n = w.shape[1]
    vec = pl.BlockSpec((1, d), lambda i, j: (0, 0))
    return pl.pallas_call(
        functools.partial(_inproj_kernel, n_heads=n_heads),
        grid=(t // tm, n // tn),
        in_specs=[pl.BlockSpec((tm, d), lambda i, j: (i, 0)), vec, vec, vec,
                  pl.BlockSpec((d, tn), lambda i, j: (0, j)),
                  pl.BlockSpec((d, LANES), lambda i, j: (0, 0))],
        out_specs=[pl.BlockSpec((tm, tn), lambda i, j: (i, j)),
                   pl.BlockSpec((tm, LANES), lambda i, j: (i, 0))],
        out_shape=[jax.ShapeDtypeStruct((t, n), F32), jax.ShapeDtypeStruct((t, LANES), F32)],
        scratch_shapes=[pltpu.VMEM((tm, d), BF16)],
        compiler_params=_cparams(("arbitrary", "arbitrary")),
        name="in_proj",
    )(x, g, sc, sh, w, wdt)


def _ssd_pool_kernel(xbc_ref, xbch_ref, z0_ref, z1_ref, z2_ref, u_ref, uh_ref, dt_ref,
                     cw_ref, cb_ref, dtb_ref, alog_ref, dskip_ref, ng_ref, eh_ref,
                     wp_ref, bp_ref, ps_ref, y_ref,
                     st_scr, xe_scr, xa_scr, ue_scr, yd_scr, *, d_ssd, n_heads):
    i = pl.program_id(0)
    L = CHUNK
    gw = d_ssd // SSD_GROUPS
    halo_c = xe_scr.shape[0] - L
    halo_u = ue_scr.shape[0] - L
    first = i == 0

    @pl.when(first)
    def _():
        st_scr[...] = jnp.zeros_like(st_scr)

    xe_scr[0:halo_c, :] = jnp.where(first, 0.0, xbch_ref[...])
    xe_scr[halo_c:, :] = xbc_ref[...]
    ncols = xe_scr.shape[1]
    cblk = 512
    for c0 in range(0, ncols, cblk):
        acc = jnp.broadcast_to(cb_ref[:, c0:c0 + cblk], (L, cblk))
        for k in range(CONV_K):
            tap = xe_scr[pl.ds(halo_c - (CONV_K - 1) + k, L), c0:c0 + cblk]
            acc = acc + cw_ref[k:k + 1, c0:c0 + cblk] * tap
        xa_scr[:, c0:c0 + cblk] = _silu(acc)

    lane = lax.broadcasted_iota(jnp.int32, (L, LANES), 1)
    row = lax.broadcasted_iota(jnp.int32, (L, LANES), 0)
    head_ok = lane < n_heads
    dt = jnp.where(head_ok, jax.nn.softplus(dt_ref[...] + dtb_ref[...]), 0.0)
    a = -jnp.exp(alog_ref[...])
    da = dt * a
    causal = row >= lane
    tri = jnp.where(causal, 1.0, 0.0).astype(F32)
    cum = jnp.dot(tri, da, preferred_element_type=F32, precision=HIGHEST)
    cum_t = cum.T
    dt_t = dt.T
    cum_last = cum[L - 1:L, :]
    ecum = jnp.exp(cum)
    wend = jnp.exp(cum_last - cum) * dt
    eh = eh_ref[...]
    ecum_x = jnp.dot(ecum.astype(BF16), eh, preferred_element_type=F32)
    wend_x = jnp.dot(wend.astype(BF16), eh, preferred_element_type=F32)

    hpg = n_heads // SSD_GROUPS
    for g in range(SSD_GROUPS):
        b_g = xa_scr[:, d_ssd + g * D_STATE: d_ssd + (g + 1) * D_STATE]
        c_g = xa_scr[:, d_ssd + SSD_GROUPS * D_STATE + g * D_STATE:
                     d_ssd + SSD_GROUPS * D_STATE + (g + 1) * D_STATE]
        b_bf = b_g.astype(BF16)
        c_bf = c_g.astype(BF16)
        cb = lax.dot_general(c_bf, b_bf, (((1,), (1,)), ((), ())), preferred_element_type=F32)
        for hp in range(hpg // 2):
            h0 = g * hpg + 2 * hp
            ms = []
            for h in (h0, h0 + 1):
                seg = cum[:, h:h + 1] - cum_t[h:h + 1, :]
                decay = jnp.exp(jnp.where(causal, seg, -jnp.inf))
                ms.append(cb * decay * dt_t[h:h + 1, :])
            lhs = jnp.concatenate(ms, axis=1).astype(BF16)
            c0 = h0 * SSD_HEAD_DIM
            xpair = xa_scr[:, c0:c0 + LANES]
            top = jnp.where(lane < SSD_HEAD_DIM, xpair, 0.0)
            bot = jnp.where(lane >= SSD_HEAD_DIM, xpair, 0.0)
            rhs = jnp.concatenate([top, bot], axis=0).astype(BF16)
            yd_scr[:, c0:c0 + LANES] = jnp.dot(lhs, rhs, preferred_element_type=F32)
        gs = slice(g * gw, (g + 1) * gw)
        st = st_scr[g]
        y_off = jnp.dot(c_bf, st.astype(BF16), preferred_element_type=F32) * ecum_x[:, gs]
        xs_g = xa_scr[:, gs]
        y_g = yd_scr[:, gs] + y_off + dskip_ref[:, gs] * xs_g
        xw = (xs_g * wend_x[:, gs]).astype(BF16)
        st_scr[g] = st * ecum_x[L - 1:L, gs] + jnp.dot(b_g.T.astype(BF16), xw, preferred_element_type=F32)
        zparts = (z0_ref, z1_ref, z2_ref)
        zw = z0_ref.shape[1]
        zg = jnp.concatenate(
            [zparts[(g * gw + o) // zw][:, (g * gw + o) % zw:(g * gw + o) % zw + LANES] for o in range(0, gw, LANES)],
            axis=1)
        y_g = y_g * _silu(zg)
        msq = jnp.mean(y_g * y_g, axis=-1, keepdims=True)
        y_ref[:, gs] = (y_g * lax.rsqrt(msq + NORM_EPS) * ng_ref[:, gs]).astype(y_ref.dtype)

    ue_scr[0:halo_u, :] = jnp.where(first, 0.0, uh_ref[...])
    ue_scr[halo_u:, :] = u_ref[...]
    pg = ue_scr.shape[1] // len(POOL_WINDOWS)
    tpos = (i * L + lax.broadcasted_iota(jnp.int32, (L, pg), 0) + 1).astype(F32)
    for gi, w in enumerate(POOL_WINDOWS):
        cs = slice(gi * pg, (gi + 1) * pg)
        tok = ue_scr[halo_u:, cs]
        win = tok
        for d in range(1, w):
            win = win + ue_scr[pl.ds(halo_u - d, L), cs]
        pooled = win / jnp.minimum(tpos, float(w)) - tok
        yp = jnp.dot(pooled.astype(BF16), wp_ref[gi], preferred_element_type=F32)
        y_ref[:, d_ssd + gi * pg: d_ssd + (gi + 1) * pg] = ((yp + bp_ref[:, cs]) * ps_ref[:, cs]).astype(y_ref.dtype)


def _ssd_pool(proj, dt, cw, cb, dtb, alog, dskip_x, ng, eh, wp, bp, ps, d_ssd, d_conv, d_pool, n_heads):
    t = proj.shape[0]
    L = CHUNK
    halo_c, halo_u = 8, 16
    zw = 1024
    zb = d_conv // zw
    ub = (d_conv + d_ssd) // d_pool
    d_mix = d_ssd + d_pool
    gw = d_ssd // SSD_GROUPS

    def full(shape):
        return pl.BlockSpec(shape, lambda i: (0,) * len(shape))

    in_specs = [
        pl.BlockSpec((L, d_conv), lambda i: (i, 0)),
        pl.BlockSpec((halo_c, d_conv), lambda i: (jnp.maximum(i * (L // halo_c) - 1, 0), 0)),
        pl.BlockSpec((L, zw), lambda i: (i, zb)),
        pl.BlockSpec((L, zw), lambda i: (i, zb + 1)),
        pl.BlockSpec((L, zw), lambda i: (i, zb + 2)),
        pl.BlockSpec((L, d_pool), lambda i: (i, ub)),
        pl.BlockSpec((halo_u, d_pool), lambda i: (jnp.maximum(i * (L // halo_u) - 1, 0), ub)),
        pl.BlockSpec((L, LANES), lambda i: (i, 0)),
        full(cw.shape), full(cb.shape), full(dtb.shape), full(alog.shape), full(dskip_x.shape),
        full(ng.shape), full(eh.shape), full(wp.shape), full(bp.shape), full(ps.shape),
    ]
    kern = functools.partial(_ssd_pool_kernel, d_ssd=d_ssd, n_heads=n_heads)
    return pl.pallas_call(
        kern,
        grid=(t // L,),
        in_specs=in_specs,
        out_specs=pl.BlockSpec((L, d_mix), lambda i: (i, 0)),
        out_shape=jax.ShapeDtypeStruct((t, d_mix), BF16),
        scratch_shapes=[pltpu.VMEM((SSD_GROUPS, D_STATE, gw), F32),
                        pltpu.VMEM((L + halo_c, d_conv), F32),
                        pltpu.VMEM((L, d_conv), F32),
                        pltpu.VMEM((L + halo_u, d_pool), F32),
                        pltpu.VMEM((L, d_ssd), F32)],
        compiler_params=_cparams(("arbitrary",)),
        name="ssd_pool",
    )(proj, proj, proj, proj, proj, proj, proj, dt, cw, cb, dtb, alog, dskip_x, ng, eh, wp, bp, ps)


def _pack_bf16_pairs(lo, hi):
    lo_b = pltpu.bitcast(lo.astype(BF16).astype(F32), jnp.uint32) >> 16
    hi_b = pltpu.bitcast(hi.astype(BF16).astype(F32), jnp.uint32) & jnp.uint32(0xFFFF0000)
    return lo_b | hi_b


def _unpack_bf16_pairs(w):
    lo = pltpu.bitcast(w << 16, F32).astype(BF16)
    hi = pltpu.bitcast(w & jnp.uint32(0xFFFF0000), F32).astype(BF16)
    return lo, hi


def _outproj_kernel(y_ref, x_ref, w_ref, g1_ref, ng_ref, sc_ref, sh_ref, wr_ref, br_ref,
                    x1_ref, h2p_ref, idx_ref, gate_ref, *, n_experts):
    mix = jnp.dot(y_ref[...], w_ref[...], preferred_element_type=F32)
    x1 = x_ref[...] + g1_ref[...] * mix
    x1_ref[...] = x1
    ms = jnp.mean(x1 * x1, axis=-1, keepdims=True)
    h = x1 * lax.rsqrt(ms + NORM_EPS) * ng_ref[...]
    h = h * (1.0 + sc_ref[...]) + sh_ref[...]
    half = h.shape[1] // 2
    h2p_ref[...] = _pack_bf16_pairs(h[:, :half], h[:, half:])

    h_hi = h.astype(BF16)
    h_mid = (h - h_hi.astype(F32)).astype(BF16)
    wr = wr_ref[...]
    r_hi = jnp.dot(h_hi, wr, preferred_element_type=F32)
    r_mid = jnp.dot(h_mid, wr, preferred_element_type=F32)
    logits = (r_hi + r_mid) + pltpu.roll(r_hi, LANES - n_experts, axis=1) + br_ref[...]
    tm = logits.shape[0]
    lane = lax.broadcasted_iota(jnp.int32, (tm, LANES), 1)
    lane_f = lane.astype(F32)
    vals = jnp.where(lane < n_experts, logits, -jnp.inf)
    top_v, top_i = [], []
    for _ in range(TOP_K):
        m = jnp.max(vals, axis=-1, keepdims=True)
        am = jnp.min(jnp.where(vals == m, lane_f, float(LANES)), axis=-1, keepdims=True)
        top_v.append(m)
        top_i.append(am)
        vals = jnp.where(lane_f == am, -jnp.inf, vals)
    es = [jnp.exp(v - top_v[0]) for v in top_v]
    denom = es[0] + es[1] + es[2] + es[3]
    idx_out = jnp.zeros((tm, LANES), F32)
    gate_out = jnp.zeros((tm, LANES), F32)
    for k in range(TOP_K):
        idx_out = jnp.where(lane == k, top_i[k], idx_out)
        gate_out = jnp.where(lane == k, es[k] / denom, gate_out)
    idx_ref[...] = idx_out.astype(jnp.int32)
    gate_ref[...] = gate_out


def _out_proj(ycat, x, w, g1, ng, sc, sh, wr, br, n_experts, tm):
    t, d = x.shape
    dm = ycat.shape[1]
    vec = pl.BlockSpec((1, d), lambda i: (0, 0))
    kern = functools.partial(_outproj_kernel, n_experts=n_experts)
    return pl.pallas_call(
        kern,
        grid=(t // tm,),
        in_specs=[pl.BlockSpec((tm, dm), lambda i: (i, 0)),
                  pl.BlockSpec((tm, d), lambda i: (i, 0)),
                  pl.BlockSpec((dm, d), lambda i: (0, 0), pipeline_mode=pl.Buffered(1)),
                  vec, vec, vec, vec,
                  pl.BlockSpec((d, LANES), lambda i: (0, 0)),
                  pl.BlockSpec((1, LANES), lambda i: (0, 0))],
        out_specs=[pl.BlockSpec((tm, d), lambda i: (i, 0)),
                   pl.BlockSpec((tm, d // 2), lambda i: (i, 0)),
                   pl.BlockSpec((tm, LANES), lambda i: (i, 0)),
                   pl.BlockSpec((tm, LANES), lambda i: (i, 0))],
        out_shape=[jax.ShapeDtypeStruct((t, d), F32),
                   jax.ShapeDtypeStruct((t, d // 2), jnp.uint32),
                   jax.ShapeDtypeStruct((t, LANES), jnp.int32),
                   jax.ShapeDtypeStruct((t, LANES), F32)],
        compiler_params=_cparams(("arbitrary",)),
        name="out_proj",
    )(ycat, x, w, g1, ng, sc, sh, wr, br)


def _first_half(x, c, w_ada, b_ada, norm1_g, w_in_proj, conv_w, conv_b, dt_bias, a_log, d_skip, ssd_norm_g,
                w_pool, b_pool, pool_scale, w_out_proj, norm2_g, w_router, b_router):
    _, t, d = x.shape
    n_heads = dt_bias.shape[1]
    d_ssd = n_heads * SSD_HEAD_DIM
    d_conv = conv_w.shape[2]
    d_pool = b_pool.shape[1]
    n_experts = w_router.shape[2]
    x2 = x.reshape(t, d)

    mod = _ada(c, w_ada, b_ada)
    sh1, sc1, g1, sh2, sc2, g2 = [mod[:, k * d:(k + 1) * d] for k in range(6)]

    wi = w_in_proj[0]
    w_main = jnp.concatenate([wi[:, d_ssd:d_ssd + d_conv], wi[:, :d_ssd], wi[:, d_ssd + d_conv + n_heads:]],
                             axis=1).astype(BF16)
    w_dt32 = wi[:, d_ssd + d_conv:d_ssd + d_conv + n_heads]
    w_dt_hi = w_dt32.astype(BF16)
    w_dt_mid = (w_dt32 - w_dt_hi.astype(F32)).astype(BF16)
    assert 2 * n_heads <= LANES
    w_dt = jnp.pad(jnp.concatenate([w_dt_hi, w_dt_mid], axis=1), ((0, 0), (0, LANES - 2 * n_heads)))
    tm1 = min(1024, t)
    proj, dt_raw = _in_proj(x2, norm1_g, sc1, sh1, w_main, w_dt, tm1, 1024, n_heads)

    pad_h = LANES - n_heads
    dtb = jnp.pad(dt_bias, ((0, 0), (0, pad_h)))
    alog = jnp.pad(a_log, ((0, 0), (0, pad_h)))
    dskip_x = jnp.repeat(d_skip, SSD_HEAD_DIM, axis=1)
    eh = (jnp.arange(LANES)[:, None] == (jnp.arange(d_ssd) // SSD_HEAD_DIM)[None, :]).astype(BF16)
    ycat = _ssd_pool(proj, dt_raw, conv_w[0], conv_b, dtb, alog, dskip_x, ssd_norm_g, eh,
                     w_pool[0].astype(BF16), b_pool, pool_scale, d_ssd, d_conv, d_pool, n_heads)

    wr_hi = w_router[0].astype(BF16)
    wr_mid = (w_router[0] - wr_hi.astype(F32)).astype(BF16)
    assert 2 * n_experts <= LANES
    wr = jnp.pad(jnp.concatenate([wr_hi, wr_mid], axis=1), ((0, 0), (0, LANES - 2 * n_experts)))
    br = jnp.pad(b_router, ((0, 0), (0, LANES - n_experts)))
    x1, h2p, idx, gates = _out_proj(ycat, x2, w_out_proj[0].astype(BF16), g1, norm2_g, sc2, sh2, wr, br,
                                    n_experts, min(512, t))
    return x1, h2p, idx, gates, g2


MOE_SUB = 256
MOE_ROWS_MAX = 2304
MOE_TF = 256
MOE_TAB_ALIGN = 1024
MOE_TAB = 4096


def _route_tables(idx, n_experts):
    t, k = idx.shape
    rmax, sub = MOE_ROWS_MAX, MOE_SUB
    ns_max = n_experts + (t * k + rmax - 1) // rmax
    tok = jnp.arange(t, dtype=jnp.int32)[:, None]
    key = (idx * t + tok) * k + jnp.arange(k, dtype=jnp.int32)[None, :]
    skey = jnp.sort(key.reshape(-1))
    row_tok = (skey // k) % t
    row_dst = (skey % k) * t + row_tok
    bounds = jnp.searchsorted(skey, jnp.arange(n_experts + 1, dtype=jnp.int32) * (t * k)).astype(jnp.int32)
    off, counts = bounds[:-1], bounds[1:] - bounds[:-1]
    n_sup = (counts + rmax - 1) // rmax
    per = (counts + jnp.maximum(n_sup, 1) - 1) // jnp.maximum(n_sup, 1)
    rps = jnp.maximum((per + sub - 1) // sub * sub, sub)
    sup_end = jnp.cumsum(n_sup)
    sup_base = sup_end - n_sup
    s_ids = jnp.arange(ns_max + 1, dtype=jnp.int32)
    live = s_ids < sup_end[-1]
    st_e = jnp.minimum(jnp.searchsorted(sup_end, s_ids, side="right"), n_experts - 1).astype(jnp.int32)
    s_loc = s_ids - sup_base[st_e]
    st_n = jnp.where(live, jnp.clip(counts[st_e] - s_loc * rps[st_e], 0, rps[st_e]), 0).astype(jnp.int32)
    st_start = jnp.where(live, off[st_e] + s_loc * rps[st_e], 0).astype(jnp.int32)
    n_super = sup_end[-1:].astype(jnp.int32)
    row_tok = jnp.pad(row_tok, (0, MOE_TAB))
    row_dst = jnp.pad(row_dst, (0, MOE_TAB))
    return row_tok, row_dst, st_e, st_n, st_start, n_super


def _moe_kernel(st_e, st_n, st_start, nsup, tok_hbm, dst_hbm, h2p_hbm, win_hbm, wout_hbm, bin_ref, bout_ref,
                perm_ref, y_hbm, tok_s, dst_s, xbuf, acc, ybuf, win_st, wout_st, wi_bf, wo_bf, act_scr,
                tsem, gsem, ssem, wsem, *, nf):
    sub, rmax, tf = MOE_SUB, MOE_ROWS_MAX, MOE_TF
    nsub_max = rmax // sub
    gsub = sub // SUBLANES
    dh = xbuf.shape[2]
    d_out = acc.shape[2]
    n_super = nsup[0]

    def n_sub_of(s):
        return (st_n[s] + sub - 1) // sub

    def tab_base(s):
        return pl.multiple_of((st_start[s] // MOE_TAB_ALIGN) * MOE_TAB_ALIGN, MOE_TAB_ALIGN)

    def tab_off(s, slot):
        return slot * MOE_TAB + st_start[s] - tab_base(s)

    def table_copies(s, slot):
        win = pl.ds(tab_base(s), MOE_TAB)
        dst = pl.ds(pl.multiple_of(slot * MOE_TAB, MOE_TAB), MOE_TAB)
        return (pltpu.make_async_copy(tok_hbm.at[win], tok_s.at[dst], tsem.at[slot]),
                pltpu.make_async_copy(dst_hbm.at[win], dst_s.at[dst], tsem.at[slot]))

    def groups(m):
        return pl.ds(pl.multiple_of(m * gsub, gsub), gsub)

    def weight_copies(e, j):
        cols = pl.ds(pl.multiple_of(j * 2 * tf, 2 * tf), 2 * tf)
        hk = win_st.shape[0] // 2
        c_in0 = pltpu.make_async_copy(win_hbm.at[e, 0:hk, cols], win_st.at[0:hk], wsem.at[0])
        c_in1 = pltpu.make_async_copy(win_hbm.at[e, hk:2 * hk, cols], win_st.at[hk:2 * hk], wsem.at[0])
        c_out = pltpu.make_async_copy(wout_hbm.at[e, pl.ds(pl.multiple_of(j * tf, tf), tf), :],
                                      wout_st, wsem.at[0])
        return (c_in0, 0), (c_in1, 1), (c_out, 0)

    def gather_rows(m, toff):
        def body(i, carry):
            g = m * gsub + i
            for u in range(SUBLANES):
                tk = tok_s[toff + g * SUBLANES + u]
                pltpu.make_async_copy(h2p_hbm.at[pl.ds(tk, 1)], xbuf.at[g, pl.ds(u, 1)],
                                      gsem.at[m]).start(priority=u % 2)
            return carry
        lax.fori_loop(0, gsub, body, 0)

    def wait_gather(m):
        pltpu.make_async_copy(xbuf.at[groups(m)], xbuf.at[groups(m)], gsem.at[m]).wait()

    def scatter_rows(m, toff, nvalid):
        def body(i, carry):
            g = m * gsub + i
            for u in range(SUBLANES):
                d = dst_s[toff + g * SUBLANES + u]
                pltpu.make_async_copy(ybuf.at[g, pl.ds(u, 1)], y_hbm.at[pl.ds(d, 1)],
                                      ssem.at[m]).start(priority=u % 2)
            return carry
        n_full = nvalid // SUBLANES
        lax.fori_loop(0, n_full, body, 0)
        g_last = m * gsub + n_full

        def tail(u, carry):
            d = dst_s[toff + g_last * SUBLANES + u]
            pltpu.make_async_copy(ybuf.at[g_last, pl.ds(u, 1)], y_hbm.at[pl.ds(d, 1)], ssem.at[m]).start()
            return carry
        lax.fori_loop(0, nvalid - n_full * SUBLANES, tail, 0)

    def wait_scatter(m, nvalid):
        @pl.when(nvalid == sub)
        def _():
            pltpu.make_async_copy(ybuf.at[groups(m)], ybuf.at[groups(m)], ssem.at[m]).wait()

        @pl.when(nvalid < sub)
        def _():
            def body(i, carry):
                pltpu.make_async_copy(ybuf.at[0, pl.ds(0, 1)], y_hbm.at[pl.ds(0, 1)], ssem.at[m]).wait()
                return carry
            lax.fori_loop(0, nvalid, body, 0)

    def valid_rows(n_rows, m):
        return jnp.clip(n_rows - m * sub, 0, sub)

    xbuf[...] = jnp.zeros_like(xbuf)
    acc[...] = jnp.zeros_like(acc)
    for cp in table_copies(0, 0):
        cp.start()
    for cp in table_copies(0, 0):
        cp.wait()

    @pl.when(n_super > 0)
    def _():
        for cp, prio in weight_copies(st_e[0], 0):
            cp.start(priority=prio)
        toff0 = tab_off(0, 0)
        lax.fori_loop(0, n_sub_of(0), lambda m, c: (gather_rows(m, toff0), c)[1], 0)

    def supertile(s, carry):
        e = st_e[s]
        n_rows = st_n[s]
        n_sub = n_sub_of(s)
        tslot = s % 2
        toff = tab_off(s, tslot)
        n_rows_prev = jnp.where(s > 0, st_n[jnp.maximum(s - 1, 0)], 0)
        for cp in table_copies(s + 1, 1 - tslot):
            cp.start()

        def chunk(j, carry):
            for cp, _ in weight_copies(e, j):
                cp.wait()
            perm = perm_ref[...]
            rblk = 512
            for cgrp in range(2 * tf // 256):
                for rb in range(0, wi_bf.shape[0], rblk):
                    wblk = win_st[rb:rb + rblk, cgrp * 256:(cgrp + 1) * 256].astype(BF16)
                    pw = jnp.dot(wblk, perm, preferred_element_type=F32).astype(BF16)
                    wi_bf[rb:rb + rblk, cgrp * LANES:(cgrp + 1) * LANES] = pw[:, :LANES]
                    wi_bf[rb:rb + rblk, tf + cgrp * LANES: tf + (cgrp + 1) * LANES] = pw[:, LANES:]
            wo_bf[...] = wout_st[...].astype(BF16)
            b_in = bin_ref[e, pl.ds(j, 1), :]

            @pl.when(j < nf - 1)
            def _():
                for cp, prio in weight_copies(e, j + 1):
                    cp.start(priority=prio)

            @pl.when((j == nf - 1) & (s + 1 < n_super))
            def _():
                for cp, prio in weight_copies(st_e[s + 1], 0):
                    cp.start(priority=prio)

            @pl.when(j == 0)
            def _():
                lax.fori_loop(0, n_sub, lambda m, c2: (wait_gather(m), c2)[1], 0)

            act_a, act_b = act_scr.at[0], act_scr.at[1]

            def stage1(m, act_ref):
                x_lo, x_hi = _unpack_bf16_pairs(xbuf[groups(m)].reshape(sub, dh))
                hb = (jnp.dot(x_lo, wi_bf[0:dh, :], preferred_element_type=F32)
                      + jnp.dot(x_hi, wi_bf[dh:2 * dh, :], preferred_element_type=F32) + b_in)
                glu = jnp.minimum(hb[:, :tf], SWIGLU_LIMIT)
                lin = jnp.clip(hb[:, tf:], -SWIGLU_LIMIT, SWIGLU_LIMIT)
                act = glu * jax.nn.sigmoid(SWIGLU_ALPHA * glu) * (lin + 1.0)
                act_ref[...] = act.astype(BF16)

            def stage2(m, act_ref):
                o = jnp.dot(act_ref[...], wo_bf[...], preferred_element_type=F32)
                prev = acc[groups(m)].reshape(sub, d_out)
                acc[groups(m)] = (jnp.where(j > 0, prev, 0.0) + o).reshape(gsub, SUBLANES, d_out)

            stage1(0, act_a)
            n_pairs = (n_sub - 1) // 2

            def pair(p, carry):
                m = 2 * p
                stage1(m + 1, act_b)
                stage2(m, act_a)
                stage1(m + 2, act_a)
                stage2(m + 1, act_b)
                return carry
            lax.fori_loop(0, n_pairs, pair, 0)
            m_last = 2 * n_pairs

            @pl.when(m_last < n_sub - 1)
            def _():
                stage1(m_last + 1, act_b)
                stage2(m_last, act_a)
                stage2(m_last + 1, act_b)

            @pl.when(m_last == n_sub - 1)
            def _():
                stage2(m_last, act_a)
            return carry
        lax.fori_loop(0, nf, chunk, 0)

        for cp in table_copies(s + 1, 1 - tslot):
            cp.wait()
        toff_next = tab_off(s + 1, 1 - tslot)
        lax.fori_loop(0, n_sub_of(s + 1), lambda m, c: (gather_rows(m, toff_next), c)[1], 0)
        n_sub_prev = (n_rows_prev + sub - 1) // sub
        lax.fori_loop(n_sub, jnp.maximum(n_sub, n_sub_prev),
                      lambda m, c: (wait_scatter(m, valid_rows(n_rows_prev, m)), c)[1], 0)
        b_out = bout_ref[e]

        def finish(m, carry):
            wait_scatter(m, valid_rows(n_rows_prev, m))
            v = acc[groups(m)].reshape(sub, d_out) + b_out
            ybuf[groups(m)] = _pack_bf16_pairs(v[:, :dh], v[:, dh:]).reshape(gsub, SUBLANES, dh)
            scatter_rows(m, toff, valid_rows(n_rows, m))
            return carry
        lax.fori_loop(0, n_sub, finish, 0)
        return carry
    lax.fori_loop(0, n_super, supertile, 0)

    n_rows_last = jnp.where(n_super > 0, st_n[jnp.maximum(n_super - 1, 0)], 0)
    for m in range(nsub_max):
        wait_scatter(m, valid_rows(n_rows_last, m))


def _moe(h2p, row_tok, row_dst, st_e, st_n, st_start, n_super, w_in, w_out, b_in_c, b_out, perm, t):
    n_experts, d, f2 = w_in.shape
    nf = f2 // (2 * MOE_TF)
    rmax = MOE_ROWS_MAX
    nsub_max = rmax // MOE_SUB
    assert rmax % MOE_SUB == 0 and MOE_SUB % SUBLANES == 0
    assert MOE_TAB >= rmax + MOE_TAB_ALIGN and MOE_TAB % MOE_TAB_ALIGN == 0
    smem = pl.BlockSpec(memory_space=pltpu.SMEM)
    hbm = pl.BlockSpec(memory_space=pl.ANY)
    vmem = pl.BlockSpec(memory_space=pltpu.VMEM)
    kern = functools.partial(_moe_kernel, nf=nf)
    return pl.pallas_call(
        kern,
        in_specs=[smem, smem, smem, smem, hbm, hbm, hbm, hbm, hbm, vmem, vmem, vmem],
        out_specs=hbm,
        out_shape=jax.ShapeDtypeStruct((TOP_K * t, d // 2), jnp.uint32),
        scratch_shapes=[pltpu.SMEM((2 * MOE_TAB,), jnp.int32),
                        pltpu.SMEM((2 * MOE_TAB,), jnp.int32),
                        pltpu.VMEM((rmax // SUBLANES, SUBLANES, d // 2), jnp.uint32),
                        pltpu.VMEM((rmax // SUBLANES, SUBLANES, d), F32),
                        pltpu.VMEM((rmax // SUBLANES, SUBLANES, d // 2), jnp.uint32),
                        pltpu.VMEM((d, 2 * MOE_TF), F32),
                        pltpu.VMEM((MOE_TF, d), F32),
                        pltpu.VMEM((d, 2 * MOE_TF), BF16),
                        pltpu.VMEM((MOE_TF, d), BF16),
                        pltpu.VMEM((2, MOE_SUB, MOE_TF), BF16),
                        pltpu.SemaphoreType.DMA((2,)),
                        pltpu.SemaphoreType.DMA((nsub_max,)),
                        pltpu.SemaphoreType.DMA((nsub_max,)),
                        pltpu.SemaphoreType.DMA((1,))],
        compiler_params=pltpu.CompilerParams(vmem_limit_bytes=VMEM_LIMIT, has_side_effects=True),
        name="moe",
    )(st_e, st_n, st_start, n_super, row_tok, row_dst, h2p, w_in, w_out, b_in_c, b_out, perm)


def _combine_kernel(y0_ref, y1_ref, y2_ref, y3_ref, gate_ref, x1_ref, g2_ref, fg_ref, o_ref):
    g = gate_ref[...]
    half = y0_ref.shape[1]
    y_lo = jnp.zeros(y0_ref.shape, F32)
    y_hi = jnp.zeros(y0_ref.shape, F32)
    for k, y_ref in enumerate((y0_ref, y1_ref, y2_ref, y3_ref)):
        w = y_ref[...]
        y_lo = y_lo + pltpu.bitcast(w << 16, F32) * g[:, k:k + 1]
        y_hi = y_hi + pltpu.bitcast(w & jnp.uint32(0xFFFF0000), F32) * g[:, k:k + 1]
    x_lo = x1_ref[:, :half] + g2_ref[:, :half] * y_lo
    x_hi = x1_ref[:, half:] + g2_ref[:, half:] * y_hi
    ssq = jnp.sum(x_lo * x_lo, axis=-1, keepdims=True) + jnp.sum(x_hi * x_hi, axis=-1, keepdims=True)
    inv = lax.rsqrt(ssq / (2 * half) + NORM_EPS)
    o_ref[:, :half] = x_lo * inv * fg_ref[:, :half]
    o_ref[:, half:] = x_hi * inv * fg_ref[:, half:]


def _combine(y, gates, x1, g2, fg, tm):
    t, d = x1.shape
    nb = t // tm
    vec = pl.BlockSpec((1, d), lambda i: (0, 0))
    yspecs = [pl.BlockSpec((tm, d // 2), functools.partial(lambda i, k: (k * nb + i, 0), k=k))
              for k in range(TOP_K)]
    return pl.pallas_call(
        _combine_kernel,
        grid=(nb,),
        in_specs=yspecs + [pl.BlockSpec((tm, LANES), lambda i: (i, 0)),
                           pl.BlockSpec((tm, d), lambda i: (i, 0)), vec, vec],
        out_specs=pl.BlockSpec((tm, d), lambda i: (i, 0)),
        out_shape=jax.ShapeDtypeStruct((t, d), F32),
        compiler_params=_cparams(("arbitrary",)),
        name="combine",
    )(y, y, y, y, gates, x1, g2, fg)


def kernel(x, c, w_ada, b_ada, norm1_g, w_in_proj, conv_w, conv_b, dt_bias, a_log, d_skip, ssd_norm_g, w_pool,
           b_pool, pool_scale, w_out_proj, norm2_g, w_router, b_router, w_exp_in, b_exp_in, w_exp_out, b_exp_out,
           final_norm_g):
    assert x.shape[0] == 1 and w_ada.shape[0] == 1
    _, t, d = x.shape
    n_experts = w_router.shape[2]
    x1, h2p, idx, gates, g2 = _first_half(x, c, w_ada, b_ada, norm1_g, w_in_proj, conv_w, conv_b, dt_bias, a_log,
                                          d_skip, ssd_norm_g, w_pool, b_pool, pool_scale, w_out_proj, norm2_g,
                                          w_router, b_router)
    row_tok, row_dst, st_e, st_n, st_start, n_super = _route_tables(idx[:, :TOP_K], n_experts)
    f = w_exp_out.shape[2]
    nf = f // MOE_TF
    b_in = b_exp_in[0].reshape(n_experts, nf, MOE_TF, 2)
    b_in_c = jnp.concatenate([b_in[..., 0], b_in[..., 1]], axis=-1)
    src = jnp.concatenate([2 * jnp.arange(LANES), 2 * jnp.arange(LANES) + 1])
    perm = (jnp.arange(2 * LANES)[:, None] == src[None, :]).astype(BF16)
    y = _moe(h2p, row_tok, row_dst, st_e, st_n, st_start, n_super, w_exp_in.reshape(w_exp_in.shape[1:]),
             w_exp_out.reshape(w_exp_out.shape[1:]), b_in_c, b_exp_out.reshape(n_experts, 1, d), perm, t)
    out = _combine(y, gates, x1, g2, final_norm_g.reshape(1, d), min(256, t))
    return out.reshape(x.shape)
```

```python
import functools

import jax
import jax.numpy as jnp
from jax import lax
from jax.experimental import pallas as pl
from jax.experimental.pallas import tpu as pltpu

F32 = jnp.float32
BF16 = jnp.bfloat16
HIGHEST = lax.Precision.HIGHEST

SSD_HEAD_DIM = 64
SSD_GROUPS = 4
D_STATE = 128
CONV_K = 4
CHUNK = 128
POOL_WINDOWS = (2, 4, 8, 16)
TOP_K = 4
SWIGLU_LIMIT = 7.0
SWIGLU_ALPHA = 1.702
NORM_EPS = 1e-6

LANES = 128
SUBLANES = 8
VMEM_LIMIT = 56 * 1024 * 1024
MOE_VMEM_LIMIT = 60 * 1024 * 1024


def _cparams(sem, vmem=VMEM_LIMIT):
    return pltpu.CompilerParams(dimension_semantics=sem, vmem_limit_bytes=vmem)


def _silu(v):
    return v * jax.nn.sigmoid(v)


def _ada_kernel(c_ref, w_ref, b_ref, o_ref):
    cond = _silu(c_ref[...])
    cond8 = jnp.broadcast_to(cond, (SUBLANES, cond.shape[1]))
    o = jnp.dot(cond8, w_ref[...], preferred_element_type=F32, precision=HIGHEST)
    o_ref[...] = o[0:1] + b_ref[...]


def _ada(c, w, b):
    _, d, n = w.shape
    tn = 1536
    return pl.pallas_call(
        _ada_kernel,
        grid=(n // tn,),
        in_specs=[pl.BlockSpec((1, d), lambda j: (0, 0)),
                  pl.BlockSpec((None, d, tn), lambda j: (0, 0, j)),
                  pl.BlockSpec((1, tn), lambda j: (0, j))],
        out_specs=pl.BlockSpec((1, tn), lambda j: (0, j)),
        out_shape=jax.ShapeDtypeStruct((1, n), F32),
        compiler_params=_cparams(("arbitrary",)),
        name="ada",
    )(c, w, b)


def _prep_w_kernel(a_ref, tail_ref, wm_ref, wdt_ref, *, n_heads):
    j = pl.program_id(0)
    last = pl.num_programs(0) - 1

    @pl.when(j < last)
    def _():
        wm_ref[...] = a_ref[...].astype(BF16)

    @pl.when(j == last)
    def _():
        a = a_ref[...]
        tn = a.shape[1]
        lane = lax.broadcasted_iota(jnp.int32, (a.shape[0], LANES), 1)
        ra = pltpu.roll(a, tn - n_heads, axis=1)
        rt = pltpu.roll(tail_ref[...], LANES - n_heads, axis=1)
        top = jnp.where(lane < LANES - n_heads, ra[:, tn - LANES:], rt)
        wm_ref[...] = jnp.concatenate([ra[:, :tn - LANES], top], axis=1).astype(BF16)
        t0 = a[:, :LANES]
        hi = t0.astype(BF16).astype(F32)
        mid = pltpu.roll(t0 - hi, n_heads, axis=1)
        wdt_ref[...] = jnp.where(lane < n_heads, hi, jnp.where(lane < 2 * n_heads, mid, 0.0)).astype(BF16)


def _prep_w_in(w, d_ssd, d_conv, n_heads, d_pool):
    _, d, n_in = w.shape
    tn = d_pool
    assert d_ssd % tn == 0 and d_conv % tn == 0 and n_in == d_ssd + d_conv + n_heads + d_pool
    assert 2 * n_heads <= LANES and n_in % LANES == n_heads and (d_ssd + d_conv) % LANES == 0
    nz, nx = d_ssd // tn, d_conv // tn
    n_main = nz + nx + 1

    def in_blk(j):
        return jnp.where(j < nx, j + nz, jnp.where(j < nx + nz, j - nx, nx + nz))
    return pl.pallas_call(
        functools.partial(_prep_w_kernel, n_heads=n_heads),
        grid=(n_main,),
        in_specs=[pl.BlockSpec((None, d, tn), lambda j: (0, 0, in_blk(j))),
                  pl.BlockSpec((None, d, LANES), lambda j: (0, 0, (n_in - n_heads) // LANES))],
        out_specs=[pl.BlockSpec((d, tn), lambda j: (0, j)),
                   pl.BlockSpec((d, LANES), lambda j: (0, 0))],
        out_shape=[jax.ShapeDtypeStruct((d, n_main * tn), BF16), jax.ShapeDtypeStruct((d, LANES), BF16)],
        compiler_params=_cparams(("arbitrary",)),
        name="prep_w_in",
    )(w, w)


def _inproj_kernel(x_ref, g_ref, sc_ref, sh_ref, w_ref, wdt_ref, proj_ref, dt_ref, h_scr, *, n_heads):
    @pl.when(pl.program_id(1) == 0)
    def _():
        x = x_ref[...]
        ms = jnp.mean(x * x, axis=-1, keepdims=True)
        h = x * lax.rsqrt(ms + NORM_EPS) * g_ref[...]
        h = h * (1.0 + sc_ref[...]) + sh_ref[...]
        h_hi = h.astype(BF16)
        h_scr[...] = h_hi
        h_mid = (h - h_hi.astype(F32)).astype(BF16)
        wdt = wdt_ref[...]
        r_hi = jnp.dot(h_hi, wdt, preferred_element_type=F32)
        r_mid = jnp.dot(h_mid, wdt, preferred_element_type=F32)
        dt_ref[...] = (r_hi + r_mid) + pltpu.roll(r_hi, LANES - n_heads, axis=1)

    proj_ref[...] = jnp.dot(h_scr[...], w_ref[...], preferred_element_type=F32)


def _in_proj(x, g, sc, sh, w, wdt, tm, tn, n_heads):
    t, d = x.shape
    n = w.shape[1]
    vec = pl.BlockSpec((1, d), lambda i, j: (0, 0))
    return pl.pallas_call(
        functools.partial(_inproj_kernel, n_heads=n_heads),
        grid=(t // tm, n // tn),
        in_specs=[pl.BlockSpec((tm, d), lambda i, j: (i, 0)), vec, vec, vec,
                  pl.BlockSpec((d, tn), lambda i, j: (0, j)),
                  pl.BlockSpec((d, LANES), lambda i, j: (0, 0))],
        out_specs=[pl.BlockSpec((tm, tn), lambda i, j: (i, j)),
                   pl.BlockSpec((tm, LANES), lambda i, j: (i, 0))],
        out_shape=[jax.ShapeDtypeStruct((t, n), F32), jax.ShapeDtypeStruct((t, LANES), F32)],
        scratch_shapes=[pltpu.VMEM((tm, d), BF16)],
        compiler_params=_cparams(("arbitrary", "arbitrary")),
        name="in_proj",
    )(x, g, sc, sh, w, wdt)


def _ssd_pool_kernel(xbc_ref, xbch_ref, z0_ref, z1_ref, z2_ref, u_ref, uh_ref, dt_ref,
                     cw_ref, cb_ref, dtb_ref, alog_ref, dskip_ref, ng_ref, eh_ref,
                     wp_ref, bp_ref, ps_ref, y_ref,
                     st_scr, xe_scr, xa_scr, ue_scr, yd_scr, *, d_ssd, n_heads):
    i = pl.program_id(0)
    L = CHUNK
    gw = d_ssd // SSD_GROUPS
    halo_c = xe_scr.shape[0] - L
    halo_u = ue_scr.shape[0] - L
    first = i == 0

    @pl.when(first)
    def _():
        st_scr[...] = jnp.zeros_like(st_scr)

    xe_scr[0:halo_c, :] = jnp.where(first, 0.0, xbch_ref[...])
    xe_scr[halo_c:, :] = xbc_ref[...]
    ncols = xe_scr.shape[1]
    cblk = 512
    for c0 in range(0, ncols, cblk):
        acc = jnp.broadcast_to(cb_ref[:, c0:c0 + cblk], (L, cblk))
        for k in range(CONV_K):
            tap = xe_scr[pl.ds(halo_c - (CONV_K - 1) + k, L), c0:c0 + cblk]
            acc = acc + cw_ref[k:k + 1, c0:c0 + cblk] * tap
        xa_scr[:, c0:c0 + cblk] = _silu(acc)

    lane = lax.broadcasted_iota(jnp.int32, (L, LANES), 1)
    row = lax.broadcasted_iota(jnp.int32, (L, LANES), 0)
    head_ok = lane < n_heads
    dt = jnp.where(head_ok, jax.nn.softplus(dt_ref[...] + dtb_ref[...]), 0.0)
    a = -jnp.exp(alog_ref[...])
    da = dt * a
    causal = row >= lane
    tri = jnp.where(causal, 1.0, 0.0).astype(F32)
    cum = jnp.dot(tri, da, preferred_element_type=F32, precision=HIGHEST)
    cum_t = cum.T
    dt_t = dt.T
    cum_last = cum[L - 1:L, :]
    ecum = jnp.exp(cum)
    wend = jnp.exp(cum_last - cum) * dt
    eh = eh_ref[...]
    ecum_x = jnp.dot(ecum.astype(BF16), eh, preferred_element_type=F32)
    wend_x = jnp.dot(wend.astype(BF16), eh, preferred_element_type=F32)

    hpg = n_heads // SSD_GROUPS
    for g in range(SSD_GROUPS):
        b_g = xa_scr[:, d_ssd + g * D_STATE: d_ssd + (g + 1) * D_STATE]
        c_g = xa_scr[:, d_ssd + SSD_GROUPS * D_STATE + g * D_STATE:
                     d_ssd + SSD_GROUPS * D_STATE + (g + 1) * D_STATE]
        b_bf = b_g.astype(BF16)
        c_bf = c_g.astype(BF16)
        cb = lax.dot_general(c_bf, b_bf, (((1,), (1,)), ((), ())), preferred_element_type=F32)
        for hp in range(hpg // 2):
            h0 = g * hpg + 2 * hp
            ms = []
            for h in (h0, h0 + 1):
                seg = cum[:, h:h + 1] - cum_t[h:h + 1, :]
                decay = jnp.exp(jnp.where(causal, seg, -jnp.inf))
                ms.append(cb * decay * dt_t[h:h + 1, :])
            lhs = jnp.concatenate(ms, axis=1).astype(BF16)
            c0 = h0 * SSD_HEAD_DIM
            xpair = xa_scr[:, c0:c0 + LANES]
            top = jnp.where(lane < SSD_HEAD_DIM, xpair, 0.0)
            bot = jnp.where(lane >= SSD_HEAD_DIM, xpair, 0.0)
            rhs = jnp.concatenate([top, bot], axis=0).astype(BF16)
            yd_scr[:, c0:c0 + LANES] = jnp.dot(lhs, rhs, preferred_element_type=F32)
        gs = slice(g * gw, (g + 1) * gw)
        st = st_scr[g]
        y_off = jnp.dot(c_bf, st.astype(BF16), preferred_element_type=F32) * ecum_x[:, gs]
        xs_g = xa_scr[:, gs]
        y_g = yd_scr[:, gs] + y_off + dskip_ref[:, gs] * xs_g
        xw = (xs_g * wend_x[:, gs]).astype(BF16)
        st_scr[g] = st * ecum_x[L - 1:L, gs] + jnp.dot(b_g.T.astype(BF16), xw, preferred_element_type=F32)
        zparts = (z0_ref, z1_ref, z2_ref)
        zw = z0_ref.shape[1]
        zg = jnp.concatenate(
            [zparts[(g * gw + o) // zw][:, (g * gw + o) % zw:(g * gw + o) % zw + LANES] for o in range(0, gw, LANES)],
            axis=1)
        y_g = y_g * _silu(zg)
        msq = jnp.mean(y_g * y_g, axis=-1, keepdims=True)
        y_ref[:, gs] = (y_g * lax.rsqrt(msq + NORM_EPS) * ng_ref[:, gs]).astype(y_ref.dtype)

    ue_scr[0:halo_u, :] = jnp.where(first, 0.0, uh_ref[...])
    ue_scr[halo_u:, :] = u_ref[...]
    pg = ue_scr.shape[1] // len(POOL_WINDOWS)
    tpos = (i * L + lax.broadcasted_iota(jnp.int32, (L, pg), 0) + 1).astype(F32)
    for gi, w in enumerate(POOL_WINDOWS):
        cs = slice(gi * pg, (gi + 1) * pg)
        tok = ue_scr[halo_u:, cs]
        win = tok
        for d in range(1, w):
            win = win + ue_scr[pl.ds(halo_u - d, L), cs]
        pooled = win / jnp.minimum(tpos, float(w)) - tok
        yp = jnp.dot(pooled.astype(BF16), wp_ref[gi], preferred_element_type=F32)
        y_ref[:, d_ssd + gi * pg: d_ssd + (gi + 1) * pg] = ((yp + bp_ref[:, cs]) * ps_ref[:, cs]).astype(y_ref.dtype)


def _ssd_pool(proj, dt, cw, cb, dtb, alog, dskip_x, ng, eh, wp, bp, ps, d_ssd, d_conv, d_pool, n_heads):
    t = proj.shape[0]
    L = CHUNK
    halo_c, halo_u = 8, 16
    zw = 1024
    zb = d_conv // zw
    ub = (d_conv + d_ssd) // d_pool
    d_mix = d_ssd + d_pool
    gw = d_ssd // SSD_GROUPS

    def full(shape):
        return pl.BlockSpec(shape, lambda i: (0,) * len(shape))

    in_specs = [
        pl.BlockSpec((L, d_conv), lambda i: (i, 0)),
        pl.BlockSpec((halo_c, d_conv), lambda i: (jnp.maximum(i * (L // halo_c) - 1, 0), 0)),
        pl.BlockSpec((L, zw), lambda i: (i, zb)),
        pl.BlockSpec((L, zw), lambda i: (i, zb + 1)),
        pl.BlockSpec((L, zw), lambda i: (i, zb + 2)),
        pl.BlockSpec((L, d_pool), lambda i: (i, ub)),
        pl.BlockSpec((halo_u, d_pool), lambda i: (jnp.maximum(i * (L // halo_u) - 1, 0), ub)),
        pl.BlockSpec((L, LANES), lambda i: (i, 0)),
        full(cw.shape), full(cb.shape), full(dtb.shape), full(alog.shape), full(dskip_x.shape),
        full(ng.shape), full(eh.shape), full(wp.shape), full(bp.shape), full(ps.shape),
    ]
    kern = functools.partial(_ssd_pool_kernel, d_ssd=d_ssd, n_heads=n_heads)
    return pl.pallas_call(
        kern,
        grid=(t // L,),
        in_specs=in_specs,
        out_specs=pl.BlockSpec((L, d_mix), lambda i: (i, 0)),
        out_shape=jax.ShapeDtypeStruct((t, d_mix), BF16),
        scratch_shapes=[pltpu.VMEM((SSD_GROUPS, D_STATE, gw), F32),
                        pltpu.VMEM((L + halo_c, d_conv), F32),
                        pltpu.VMEM((L, d_conv), F32),
                        pltpu.VMEM((L + halo_u, d_pool), F32),
                        pltpu.VMEM((L, d_ssd), F32)],
        compiler_params=_cparams(("arbitrary",)),
        name="ssd_pool",
    )(proj, proj, proj, proj, proj, proj, proj, dt, cw, cb, dtb, alog, dskip_x, ng, eh, wp, bp, ps)


def _pack_bf16_pairs(lo, hi):
    lo_b = pltpu.bitcast(lo.astype(BF16).astype(F32), jnp.uint32) >> 16
    hi_b = pltpu.bitcast(hi.astype(BF16).astype(F32), jnp.uint32) & jnp.uint32(0xFFFF0000)
    return lo_b | hi_b


def _unpack_bf16_pairs(w):
    lo = pltpu.bitcast(w << 16, F32).astype(BF16)
    hi = pltpu.bitcast(w & jnp.uint32(0xFFFF0000), F32).astype(BF16)
    return lo, hi


def _outproj_kernel(y_ref, x_ref, w_ref, g1_ref, ng_ref, sc_ref, sh_ref, wr_ref, br_ref,
                    x1_ref, h2p_ref, idx_ref, gate_ref, *, n_experts):
    mix = jnp.dot(y_ref[...], w_ref[...], preferred_element_type=F32)
    x1 = x_ref[...] + g1_ref[...] * mix
    x1_ref[...] = x1
    ms = jnp.mean(x1 * x1, axis=-1, keepdims=True)
    h = x1 * lax.rsqrt(ms + NORM_EPS) * ng_ref[...]
    h = h * (1.0 + sc_ref[...]) + sh_ref[...]
    half = h.shape[1] // 2
    h2p_ref[...] = _pack_bf16_pairs(h[:, :half], h[:, half:])

    h_hi = h.astype(BF16)
    h_mid = (h - h_hi.astype(F32)).astype(BF16)
    wr = wr_ref[...]
    r_hi = jnp.dot(h_hi, wr, preferred_element_type=F32)
    r_mid = jnp.dot(h_mid, wr, preferred_element_type=F32)
    logits = (r_hi + r_mid) + pltpu.roll(r_hi, LANES - n_experts, axis=1) + br_ref[...]
    tm = logits.shape[0]
    lane = lax.broadcasted_iota(jnp.int32, (tm, LANES), 1)
    lane_f = lane.astype(F32)
    vals = jnp.where(lane < n_experts, logits, -jnp.inf)
    top_v, top_i = [], []
    for _ in range(TOP_K):
        m = jnp.max(vals, axis=-1, keepdims=True)
        am = jnp.min(jnp.where(vals == m, lane_f, float(LANES)), axis=-1, keepdims=True)
        top_v.append(m)
        top_i.append(am)
        vals = jnp.where(lane_f == am, -jnp.inf, vals)
    es = [jnp.exp(v - top_v[0]) for v in top_v]
    denom = es[0] + es[1] + es[2] + es[3]
    idx_out = jnp.zeros((tm, LANES), F32)
    gate_out = jnp.zeros((tm, LANES), F32)
    for k in range(TOP_K):
        idx_out = jnp.where(lane == k, top_i[k], idx_out)
        gate_out = jnp.where(lane == k, es[k] / denom, gate_out)
    idx_ref[...] = idx_out.astype(jnp.int32)
    gate_ref[...] = gate_out


def _out_proj(ycat, x, w, g1, ng, sc, sh, wr, br, n_experts, tm):
    t, d = x.shape
    dm = ycat.shape[1]
    vec = pl.BlockSpec((1, d), lambda i: (0, 0))
    kern = functools.partial(_outproj_kernel, n_experts=n_experts)
    return pl.pallas_call(
        kern,
        grid=(t // tm,),
        in_specs=[pl.BlockSpec((tm, dm), lambda i: (i, 0)),
                  pl.BlockSpec((tm, d), lambda i: (i, 0)),
                  pl.BlockSpec((dm, d), lambda i: (0, 0), pipeline_mode=pl.Buffered(1)),
                  vec, vec, vec, vec,
                  pl.BlockSpec((d, LANES), lambda i: (0, 0)),
                  pl.BlockSpec((1, LANES), lambda i: (0, 0))],
        out_specs=[pl.BlockSpec((tm, d), lambda i: (i, 0)),
                   pl.BlockSpec((tm, d // 2), lambda i: (i, 0)),
                   pl.BlockSpec((tm, LANES), lambda i: (i, 0)),
                   pl.BlockSpec((tm, LANES), lambda i: (i, 0))],
        out_shape=[jax.ShapeDtypeStruct((t, d), F32),
                   jax.ShapeDtypeStruct((t, d // 2), jnp.uint32),
                   jax.ShapeDtypeStruct((t, LANES), jnp.int32),
                   jax.ShapeDtypeStruct((t, LANES), F32)],
        compiler_params=_cparams(("arbitrary",)),
        name="out_proj",
    )(ycat, x, w, g1, ng, sc, sh, wr, br)


def _first_half(x, c, w_ada, b_ada, norm1_g, w_in_proj, conv_w, conv_b, dt_bias, a_log, d_skip, ssd_norm_g,
                w_pool, b_pool, pool_scale, w_out_proj, norm2_g, w_router, b_router):
    _, t, d = x.shape
    n_heads = dt_bias.shape[1]
    d_ssd = n_heads * SSD_HEAD_DIM
    d_conv = conv_w.shape[2]
    d_pool = b_pool.shape[1]
    n_experts = w_router.shape[2]
    x2 = x.reshape(t, d)

    mod = _ada(c, w_ada, b_ada)
    sh1, sc1, g1, sh2, sc2, g2 = [mod[:, k * d:(k + 1) * d] for k in range(6)]

    w_main, w_dt = _prep_w_in(w_in_proj, d_ssd, d_conv, n_heads, d_pool)
    tm1 = min(1024, t)
    proj, dt_raw = _in_proj(x2, norm1_g, sc1, sh1, w_main, w_dt, tm1, 1024, n_heads)

    pad_h = LANES - n_heads
    dtb = jnp.pad(dt_bias, ((0, 0), (0, pad_h)))
    alog = jnp.pad(a_log, ((0, 0), (0, pad_h)))
    dskip_x = jnp.repeat(d_skip, SSD_HEAD_DIM, axis=1)
    eh = (jnp.arange(LANES)[:, None] == (jnp.arange(d_ssd) // SSD_HEAD_DIM)[None, :]).astype(BF16)
    ycat = _ssd_pool(proj, dt_raw, conv_w[0], conv_b, dtb, alog, dskip_x, ssd_norm_g, eh,
                     w_pool[0].astype(BF16), b_pool, pool_scale, d_ssd, d_conv, d_pool, n_heads)

    wr_hi = w_router[0].astype(BF16)
    wr_mid = (w_router[0] - wr_hi.astype(F32)).astype(BF16)
    assert 2 * n_experts <= LANES
    wr = jnp.pad(jnp.concatenate([wr_hi, wr_mid], axis=1), ((0, 0), (0, LANES - 2 * n_experts)))
    br = jnp.pad(b_router, ((0, 0), (0, LANES - n_experts)))
    x1, h2p, idx, gates = _out_proj(ycat, x2, w_out_proj[0].astype(BF16), g1, norm2_g, sc2, sh2, wr, br,
                                    n_experts, min(512, t))
    return x1, h2p, idx, gates, g2


MOE_SUB = 256
MOE_ROWS_MAX = 2304
MOE_TF = 256
MOE_TAB_ALIGN = 1024
MOE_TAB = 4096


def _route_tables(idx, n_experts):
    t, k = idx.shape
    rmax, sub = MOE_ROWS_MAX, MOE_SUB
    ns_max = n_experts + (t * k + rmax - 1) // rmax
    tok = jnp.arange(t, dtype=jnp.int32)[:, None]
    key = (idx * t + tok) * k + jnp.arange(k, dtype=jnp.int32)[None, :]
    skey = jnp.sort(key.reshape(-1))
    row_tok = (skey // k) % t
    row_dst = (skey % k) * t + row_tok
    bounds = jnp.searchsorted(skey, jnp.arange(n_experts + 1, dtype=jnp.int32) * (t * k)).astype(jnp.int32)
    off, counts = bounds[:-1], bounds[1:] - bounds[:-1]
    n_sup = (counts + rmax - 1) // rmax
    per = (counts + jnp.maximum(n_sup, 1) - 1) // jnp.maximum(n_sup, 1)
    rps = jnp.maximum((per + sub - 1) // sub * sub, sub)
    sup_end = jnp.cumsum(n_sup)
    sup_base = sup_end - n_sup
    s_ids = jnp.arange(ns_max + 1, dtype=jnp.int32)
    live = s_ids < sup_end[-1]
    st_e = jnp.minimum(jnp.searchsorted(sup_end, s_ids, side="right"), n_experts - 1).astype(jnp.int32)
    s_loc = s_ids - sup_base[st_e]
    st_n = jnp.where(live, jnp.clip(counts[st_e] - s_loc * rps[st_e], 0, rps[st_e]), 0).astype(jnp.int32)
    st_start = jnp.where(live, off[st_e] + s_loc * rps[st_e], 0).astype(jnp.int32)
    n_super = sup_end[-1:].astype(jnp.int32)
    row_tok = jnp.pad(row_tok, (0, MOE_TAB))
    row_dst = jnp.pad(row_dst, (0, MOE_TAB))
    return row_tok, row_dst, st_e, st_n, st_start, n_super


def _moe_kernel(st_e, st_n, st_start, nsup, tok_hbm, dst_hbm, h2p_hbm, win_hbm, wout_hbm, bin_ref, bout_ref,
                perm_ref, y_hbm, tok_s, dst_s, xbuf, acc, ybuf, win_st, wout_st, wi_bf0, wo_bf0, wi_bf1, wo_bf1,
                act_scr,
                tsem, gsem, ssem, wsem, *, nf):
    sub, rmax, tf = MOE_SUB, MOE_ROWS_MAX, MOE_TF
    nsub_max = rmax // sub
    gsub = sub // SUBLANES
    dh = xbuf.shape[2]
    d_out = acc.shape[2]
    n_super = nsup[0]

    def n_sub_of(s):
        return (st_n[s] + sub - 1) // sub

    def tab_base(s):
        return pl.multiple_of((st_start[s] // MOE_TAB_ALIGN) * MOE_TAB_ALIGN, MOE_TAB_ALIGN)

    def tab_off(s, slot):
        return slot * MOE_TAB + st_start[s] - tab_base(s)

    def table_copies(s, slot):
        win = pl.ds(tab_base(s), MOE_TAB)
        dst = pl.ds(pl.multiple_of(slot * MOE_TAB, MOE_TAB), MOE_TAB)
        return (pltpu.make_async_copy(tok_hbm.at[win], tok_s.at[dst], tsem.at[slot]),
                pltpu.make_async_copy(dst_hbm.at[win], dst_s.at[dst], tsem.at[slot]))

    def groups(m):
        return pl.ds(pl.multiple_of(m * gsub, gsub), gsub)

    def weight_copies(e, j):
        cols = pl.ds(pl.multiple_of(j * 2 * tf, 2 * tf), 2 * tf)
        hk = win_st.shape[0] // 2
        c_in0 = pltpu.make_async_copy(win_hbm.at[e, 0:hk, cols], win_st.at[0:hk], wsem.at[0])
        c_in1 = pltpu.make_async_copy(win_hbm.at[e, hk:2 * hk, cols], win_st.at[hk:2 * hk], wsem.at[0])
        c_out = pltpu.make_async_copy(wout_hbm.at[e, pl.ds(pl.multiple_of(j * tf, tf), tf), :],
                                      wout_st, wsem.at[0])
        return (c_in0, 0), (c_in1, 1), (c_out, 0)

    def gather_rows(m, toff):
        def body(i, carry):
            g = m * gsub + i
            for u in range(SUBLANES):
                tk = tok_s[toff + g * SUBLANES + u]
                pltpu.make_async_copy(h2p_hbm.at[pl.ds(tk, 1)], xbuf.at[g, pl.ds(u, 1)],
                                      gsem.at[m]).start(priority=u % 2)
            return carry
        lax.fori_loop(0, gsub, body, 0)

    def wait_gather(m):
        pltpu.make_async_copy(xbuf.at[groups(m)], xbuf.at[groups(m)], gsem.at[m]).wait()

    def scatter_rows(m, toff, nvalid):
        def body(i, carry):
            g = m * gsub + i
            for u in range(SUBLANES):
                d = dst_s[toff + g * SUBLANES + u]
                pltpu.make_async_copy(ybuf.at[g, pl.ds(u, 1)], y_hbm.at[pl.ds(d, 1)],
                                      ssem.at[m]).start(priority=u % 2)
            return carry
        n_full = nvalid // SUBLANES
        lax.fori_loop(0, n_full, body, 0)
        g_last = m * gsub + n_full

        def tail(u, carry):
            d = dst_s[toff + g_last * SUBLANES + u]
            pltpu.make_async_copy(ybuf.at[g_last, pl.ds(u, 1)], y_hbm.at[pl.ds(d, 1)], ssem.at[m]).start()
            return carry
        lax.fori_loop(0, nvalid - n_full * SUBLANES, tail, 0)

    def wait_scatter(m, nvalid):
        @pl.when(nvalid == sub)
        def _():
            pltpu.make_async_copy(ybuf.at[groups(m)], ybuf.at[groups(m)], ssem.at[m]).wait()

        @pl.when(nvalid < sub)
        def _():
            def body(i, carry):
                pltpu.make_async_copy(ybuf.at[0, pl.ds(0, 1)], y_hbm.at[pl.ds(0, 1)], ssem.at[m]).wait()
                return carry
            lax.fori_loop(0, nvalid, body, 0)

    def valid_rows(n_rows, m):
        return jnp.clip(n_rows - m * sub, 0, sub)

    xbuf[...] = jnp.zeros_like(xbuf)
    acc[...] = jnp.zeros_like(acc)
    for cp in table_copies(0, 0):
        cp.start()
    for cp in table_copies(0, 0):
        cp.wait()

    act_a, act_b = act_scr.at[0], act_scr.at[1]

    def succ(s, j):
        last = j == nf - 1
        s1 = jnp.where(last, jnp.minimum(s + 1, n_super - 1), s)
        j1 = jnp.where(last, jnp.where(s + 1 < n_super, 0, j), j + 1)
        return s1, j1

    wbufs = ((wi_bf0, wo_bf0), (wi_bf1, wo_bf1))

    def convert(ws):
        wi, wo = wbufs[ws]
        perm = perm_ref[...]
        rblk = 512
        for cgrp in range(2 * tf // 256):
            for rb in range(0, wi.shape[0], rblk):
                wblk = win_st[rb:rb + rblk, cgrp * 256:(cgrp + 1) * 256].astype(BF16)
                pw = jnp.dot(wblk, perm, preferred_element_type=F32).astype(BF16)
                wi[rb:rb + rblk, cgrp * LANES:(cgrp + 1) * LANES] = pw[:, :LANES]
                wi[rb:rb + rblk, tf + cgrp * LANES: tf + (cgrp + 1) * LANES] = pw[:, LANES:]
        wo[...] = wout_st[...].astype(BF16)

    def stage1(m, act_ref, ws, b_in):
        wi = wbufs[ws][0]
        x_lo, x_hi = _unpack_bf16_pairs(xbuf[groups(m)].reshape(sub, dh))
        hb = (jnp.dot(x_lo, wi[0:dh, :], preferred_element_type=F32)
              + jnp.dot(x_hi, wi[dh:2 * dh, :], preferred_element_type=F32) + b_in)
        glu = jnp.minimum(hb[:, :tf], SWIGLU_LIMIT)
        lin = jnp.clip(hb[:, tf:], -SWIGLU_LIMIT, SWIGLU_LIMIT)
        act = glu * jax.nn.sigmoid(SWIGLU_ALPHA * glu) * (lin + 1.0)
        act_ref[...] = act.astype(BF16)

    def stage2(m, act_ref, ws, j):
        o = jnp.dot(act_ref[...], wbufs[ws][1][...], preferred_element_type=F32)
        prev = acc[groups(m)].reshape(sub, d_out)
        acc[groups(m)] = (jnp.where(j > 0, prev, 0.0) + o).reshape(gsub, SUBLANES, d_out)

    for cp, prio in weight_copies(st_e[0], 0):
        cp.start(priority=prio)
    toff0 = tab_off(0, 0)
    lax.fori_loop(0, n_sub_of(0), lambda m, c: (gather_rows(m, toff0), c)[1], 0)
    for cp, _ in weight_copies(st_e[0], 0):
        cp.wait()
    convert(0)
    s_nx, j_nx = succ(0, 0)
    for cp, prio in weight_copies(st_e[s_nx], j_nx):
        cp.start(priority=prio)

    def supertile(s, carry):
        e = st_e[s]
        n_rows = st_n[s]
        n_sub = n_sub_of(s)
        tslot = s % 2
        toff = tab_off(s, tslot)
        n_rows_prev = jnp.where(s > 0, st_n[jnp.maximum(s - 1, 0)], 0)
        for cp in table_copies(s + 1, 1 - tslot):
            cp.start()
        lax.fori_loop(0, n_sub, lambda m, c2: (wait_gather(m), c2)[1], 0)
        stage1(0, act_a, 0, bin_ref[e, pl.ds(0, 1), :])

        def chunk(j, ws):
            b_in = bin_ref[e, pl.ds(j, 1), :]
            n_pairs = (n_sub - 1) // 2

            def pair(p, carry):
                m = 2 * p
                stage1(m + 1, act_b, ws, b_in)
                stage2(m, act_a, ws, j)
                stage1(m + 2, act_a, ws, b_in)
                stage2(m + 1, act_b, ws, j)
                return carry
            lax.fori_loop(0, n_pairs, pair, 0)
            m_last = 2 * n_pairs

            s1, j1 = succ(s, j)
            e1 = st_e[s1]
            for cp, _ in weight_copies(e1, j1):
                cp.wait()
            b_in1 = bin_ref[e1, pl.ds(j1, 1), :]

            @pl.when(m_last < n_sub - 1)
            def _():
                stage1(m_last + 1, act_b, ws, b_in)
                stage2(m_last, act_a, ws, j)
                stage2(m_last + 1, act_b, ws, j)
                convert(1 - ws)
                stage1(0, act_a, 1 - ws, b_in1)

            @pl.when(m_last == n_sub - 1)
            def _():
                stage2(m_last, act_a, ws, j)
                convert(1 - ws)
                stage1(0, act_a, 1 - ws, b_in1)

            s2, j2 = succ(s1, j1)
            for cp, prio in weight_copies(st_e[s2], j2):
                cp.start(priority=prio)

        def chunk_pair(jj, carry):
            chunk(2 * jj, 0)
            chunk(2 * jj + 1, 1)
            return carry
        lax.fori_loop(0, nf // 2, chunk_pair, 0)

        for cp in table_copies(s + 1, 1 - tslot):
            cp.wait()
        toff_next = tab_off(s + 1, 1 - tslot)
        lax.fori_loop(0, n_sub_of(s + 1), lambda m, c: (gather_rows(m, toff_next), c)[1], 0)
        n_sub_prev = (n_rows_prev + sub - 1) // sub
        lax.fori_loop(n_sub, jnp.maximum(n_sub, n_sub_prev),
                      lambda m, c: (wait_scatter(m, valid_rows(n_rows_prev, m)), c)[1], 0)
        b_out = bout_ref[pl.ds(e, 1), :]

        def finish(m, carry):
            wait_scatter(m, valid_rows(n_rows_prev, m))
            v = acc[groups(m)].reshape(sub, d_out) + b_out
            ybuf[groups(m)] = _pack_bf16_pairs(v[:, :dh], v[:, dh:]).reshape(gsub, SUBLANES, dh)
            scatter_rows(m, toff, valid_rows(n_rows, m))
            return carry
        lax.fori_loop(0, n_sub, finish, 0)
        return carry
    lax.fori_loop(0, n_super, supertile, 0)

    for cp, _ in weight_copies(st_e[0], 0):
        cp.wait()
    n_rows_last = jnp.where(n_super > 0, st_n[jnp.maximum(n_super - 1, 0)], 0)
    for m in range(nsub_max):
        wait_scatter(m, valid_rows(n_rows_last, m))


def _moe(h2p, row_tok, row_dst, st_e, st_n, st_start, n_super, w_in, w_out, b_in_c, b_out, perm, t):
    n_experts, d, f2 = w_in.shape
    nf = f2 // (2 * MOE_TF)
    rmax = MOE_ROWS_MAX
    nsub_max = rmax // MOE_SUB
    assert rmax % MOE_SUB == 0 and MOE_SUB % SUBLANES == 0 and nf % 2 == 0
    assert MOE_TAB >= rmax + MOE_TAB_ALIGN and MOE_TAB % MOE_TAB_ALIGN == 0
    smem = pl.BlockSpec(memory_space=pltpu.SMEM)
    hbm = pl.BlockSpec(memory_space=pl.ANY)
    vmem = pl.BlockSpec(memory_space=pltpu.VMEM)
    kern = functools.partial(_moe_kernel, nf=nf)
    return pl.pallas_call(
        kern,
        in_specs=[smem, smem, smem, smem, hbm, hbm, hbm, hbm, hbm, vmem, vmem, vmem],
        out_specs=hbm,
        out_shape=jax.ShapeDtypeStruct((TOP_K * t, d // 2), jnp.uint32),
        scratch_shapes=[pltpu.SMEM((2 * MOE_TAB,), jnp.int32),
                        pltpu.SMEM((2 * MOE_TAB,), jnp.int32),
                        pltpu.VMEM((rmax // SUBLANES, SUBLANES, d // 2), jnp.uint32),
                        pltpu.VMEM((rmax // SUBLANES, SUBLANES, d), F32),
                        pltpu.VMEM((rmax // SUBLANES, SUBLANES, d // 2), jnp.uint32),
                        pltpu.VMEM((d, 2 * MOE_TF), F32),
                        pltpu.VMEM((MOE_TF, d), F32),
                        pltpu.VMEM((d, 2 * MOE_TF), BF16),
                        pltpu.VMEM((MOE_TF, d), BF16),
                        pltpu.VMEM((d, 2 * MOE_TF), BF16),
                        pltpu.VMEM((MOE_TF, d), BF16),
                        pltpu.VMEM((2, MOE_SUB, MOE_TF), BF16),
                        pltpu.SemaphoreType.DMA((2,)),
                        pltpu.SemaphoreType.DMA((nsub_max,)),
                        pltpu.SemaphoreType.DMA((nsub_max,)),
                        pltpu.SemaphoreType.DMA((1,))],
        compiler_params=pltpu.CompilerParams(vmem_limit_bytes=MOE_VMEM_LIMIT, has_side_effects=True),
        name="moe",
    )(st_e, st_n, st_start, n_super, row_tok, row_dst, h2p, w_in, w_out, b_in_c, b_out, perm)


def _combine_kernel(y0_ref, y1_ref, y2_ref, y3_ref, gate_ref, x1_ref, g2_ref, fg_ref, o_ref):
    g = gate_ref[...]
    half = y0_ref.shape[1]
    y_lo = jnp.zeros(y0_ref.shape, F32)
    y_hi = jnp.zeros(y0_ref.shape, F32)
    for k, y_ref in enumerate((y0_ref, y1_ref, y2_ref, y3_ref)):
        w = y_ref[...]
        y_lo = y_lo + pltpu.bitcast(w << 16, F32) * g[:, k:k + 1]
        y_hi = y_hi + pltpu.bitcast(w & jnp.uint32(0xFFFF0000), F32) * g[:, k:k + 1]
    x_lo = x1_ref[:, :half] + g2_ref[:, :half] * y_lo
    x_hi = x1_ref[:, half:] + g2_ref[:, half:] * y_hi
    ssq = jnp.sum(x_lo * x_lo, axis=-1, keepdims=True) + jnp.sum(x_hi * x_hi, axis=-1, keepdims=True)
    inv = lax.rsqrt(ssq / (2 * half) + NORM_EPS)
    o_ref[:, :half] = x_lo * inv * fg_ref[:, :half]
    o_ref[:, half:] = x_hi * inv * fg_ref[:, half:]


def _combine(y, gates, x1, g2, fg, tm):
    t, d = x1.shape
    nb = t // tm
    vec = pl.BlockSpec((1, d), lambda i: (0, 0))
    yspecs = [pl.BlockSpec((tm, d // 2), functools.partial(lambda i, k: (k * nb + i, 0), k=k))
              for k in range(TOP_K)]
    return pl.pallas_call(
        _combine_kernel,
        grid=(nb,),
        in_specs=yspecs + [pl.BlockSpec((tm, LANES), lambda i: (i, 0)),
                           pl.BlockSpec((tm, d), lambda i: (i, 0)), vec, vec],
        out_specs=pl.BlockSpec((tm, d), lambda i: (i, 0)),
        out_shape=jax.ShapeDtypeStruct((t, d), F32),
        compiler_params=_cparams(("arbitrary",)),
        name="combine",
    )(y, y, y, y, gates, x1, g2, fg)


def kernel(x, c, w_ada, b_ada, norm1_g, w_in_proj, conv_w, conv_b, dt_bias, a_log, d_skip, ssd_norm_g, w_pool,
           b_pool, pool_scale, w_out_proj, norm2_g, w_router, b_router, w_exp_in, b_exp_in, w_exp_out, b_exp_out,
           final_norm_g):
    assert x.shape[0] == 1 and w_ada.shape[0] == 1
    _, t, d = x.shape
    n_experts = w_router.shape[2]
    x1, h2p, idx, gates, g2 = _first_half(x, c, w_ada, b_ada, norm1_g, w_in_proj, conv_w, conv_b, dt_bias, a_log,
                                          d_skip, ssd_norm_g, w_pool, b_pool, pool_scale, w_out_proj, norm2_g,
                                          w_router, b_router)
    row_tok, row_dst, st_e, st_n, st_start, n_super = _route_tables(idx[:, :TOP_K], n_experts)
    f = w_exp_out.shape[2]
    nf = f // MOE_TF
    b_in = b_exp_in[0].reshape(n_experts, nf, MOE_TF, 2)
    b_in_c = jnp.concatenate([b_in[..., 0], b_in[..., 1]], axis=-1)
    src = jnp.concatenate([2 * jnp.arange(LANES), 2 * jnp.arange(LANES) + 1])
    perm = (jnp.arange(2 * LANES)[:, None] == src[None, :]).astype(BF16)
    y = _moe(h2p, row_tok, row_dst, st_e, st_n, st_start, n_super, w_exp_in.reshape(w_exp_in.shape[1:]),
             w_exp_out.reshape(w_exp_out.shape[1:]), b_in_c, b_exp_out.reshape(n_experts, d), perm, t)
    out = _combine(y, gates, x1, g2, final_norm_g.reshape(1, d), min(256, t))
    return out.reshape(x.shape)
```

```python
import functools

import jax
import jax.numpy as jnp
from jax import lax
from jax.experimental import pallas as pl
from jax.experimental.pallas import tpu as pltpu

F32 = jnp.float32
BF16 = jnp.bfloat16
HIGHEST = lax.Precision.HIGHEST

SSD_HEAD_DIM = 64
SSD_GROUPS = 4
D_STATE = 128
CONV_K = 4
CHUNK = 128
POOL_WINDOWS = (2, 4, 8, 16)
TOP_K = 4
SWIGLU_LIMIT = 7.0
SWIGLU_ALPHA = 1.702
NORM_EPS = 1e-6

LANES = 128
SUBLANES = 8
VMEM_LIMIT = 56 * 1024 * 1024
MOE_VMEM_LIMIT = 60 * 1024 * 1024


def _cparams(sem, vmem=VMEM_LIMIT):
    return pltpu.CompilerParams(dimension_semantics=sem, vmem_limit_bytes=vmem)


def _silu(v):
    return v * jax.nn.sigmoid(v)


def _ada_kernel(c_ref, w_ref, b_ref, o_ref):
    cond = _silu(c_ref[...])
    cond8 = jnp.broadcast_to(cond, (SUBLANES, cond.shape[1]))
    o = jnp.dot(cond8, w_ref[...], preferred_element_type=F32, precision=HIGHEST)
    o_ref[...] = o[0:1] + b_ref[...]


def _ada(c, w, b):
    _, d, n = w.shape
    tn = 1536
    return pl.pallas_call(
        _ada_kernel,
        grid=(n // tn,),
        in_specs=[pl.BlockSpec((1, d), lambda j: (0, 0)),
                  pl.BlockSpec((None, d, tn), lambda j: (0, 0, j)),
                  pl.BlockSpec((1, tn), lambda j: (0, j))],
        out_specs=pl.BlockSpec((1, tn), lambda j: (0, j)),
        out_shape=jax.ShapeDtypeStruct((1, n), F32),
        compiler_params=_cparams(("arbitrary",)),
        name="ada",
    )(c, w, b)


def _prep_w_kernel(a_ref, tail_ref, wm_ref, wdt_ref, *, n_heads):
    j = pl.program_id(0)
    last = pl.num_programs(0) - 1

    @pl.when(j < last)
    def _():
        wm_ref[...] = a_ref[...].astype(BF16)

    @pl.when(j == last)
    def _():
        a = a_ref[...]
        tn = a.shape[1]
        lane = lax.broadcasted_iota(jnp.int32, (a.shape[0], LANES), 1)
        ra = pltpu.roll(a, tn - n_heads, axis=1)
        rt = pltpu.roll(tail_ref[...], LANES - n_heads, axis=1)
        top = jnp.where(lane < LANES - n_heads, ra[:, tn - LANES:], rt)
        wm_ref[...] = jnp.concatenate([ra[:, :tn - LANES], top], axis=1).astype(BF16)
        t0 = a[:, :LANES]
        hi = t0.astype(BF16).astype(F32)
        mid = pltpu.roll(t0 - hi, n_heads, axis=1)
        wdt_ref[...] = jnp.where(lane < n_heads, hi, jnp.where(lane < 2 * n_heads, mid, 0.0)).astype(BF16)


def _prep_w_in(w, d_ssd, d_conv, n_heads, d_pool):
    _, d, n_in = w.shape
    tn = d_pool
    assert d_ssd % tn == 0 and d_conv % tn == 0 and n_in == d_ssd + d_conv + n_heads + d_pool
    assert 2 * n_heads <= LANES and n_in % LANES == n_heads and (d_ssd + d_conv) % LANES == 0
    nz, nx = d_ssd // tn, d_conv // tn
    n_main = nz + nx + 1

    def in_blk(j):
        return jnp.where(j < nx, j + nz, jnp.where(j < nx + nz, j - nx, nx + nz))
    return pl.pallas_call(
        functools.partial(_prep_w_kernel, n_heads=n_heads),
        grid=(n_main,),
        in_specs=[pl.BlockSpec((None, d, tn), lambda j: (0, 0, in_blk(j))),
                  pl.BlockSpec((None, d, LANES), lambda j: (0, 0, (n_in - n_heads) // LANES))],
        out_specs=[pl.BlockSpec((d, tn), lambda j: (0, j)),
                   pl.BlockSpec((d, LANES), lambda j: (0, 0))],
        out_shape=[jax.ShapeDtypeStruct((d, n_main * tn), BF16), jax.ShapeDtypeStruct((d, LANES), BF16)],
        compiler_params=_cparams(("arbitrary",)),
        name="prep_w_in",
    )(w, w)


def _inproj_kernel(x_ref, g_ref, sc_ref, sh_ref, w_ref, wdt_ref, proj_ref, dt_ref, h_scr, *, n_heads):
    @pl.when(pl.program_id(1) == 0)
    def _():
        x = x_ref[...]
        ms = jnp.mean(x * x, axis=-1, keepdims=True)
        h = x * lax.rsqrt(ms + NORM_EPS) * g_ref[...]
        h = h * (1.0 + sc_ref[...]) + sh_ref[...]
        h_hi = h.astype(BF16)
        h_scr[...] = h_hi
        h_mid = (h - h_hi.astype(F32)).astype(BF16)
        wdt = wdt_ref[...]
        r_hi = jnp.dot(h_hi, wdt, preferred_element_type=F32)
        r_mid = jnp.dot(h_mid, wdt, preferred_element_type=F32)
        dt_ref[...] = (r_hi + r_mid) + pltpu.roll(r_hi, LANES - n_heads, axis=1)

    proj_ref[...] = jnp.dot(h_scr[...], w_ref[...], preferred_element_type=F32)


def _in_proj(x, g, sc, sh, w, wdt, tm, tn, n_heads):
    t, d = x.shape
    n = w.shape[1]
    vec = pl.BlockSpec((1, d), lambda i, j: (0, 0))
    return pl.pallas_call(
        functools.partial(_inproj_kernel, n_heads=n_heads),
        grid=(t // tm, n // tn),
        in_specs=[pl.BlockSpec((tm, d), lambda i, j: (i, 0)), vec, vec, vec,
                  pl.BlockSpec((d, tn), lambda i, j: (0, j)),
                  pl.BlockSpec((d, LANES), lambda i, j: (0, 0))],
        out_specs=[pl.BlockSpec((tm, tn), lambda i, j: (i, j)),
                   pl.BlockSpec((tm, LANES), lambda i, j: (i, 0))],
        out_shape=[jax.ShapeDtypeStruct((t, n), F32), jax.ShapeDtypeStruct((t, LANES), F32)],
        scratch_shapes=[pltpu.VMEM((tm, d), BF16)],
        compiler_params=_cparams(("arbitrary", "arbitrary")),
        name="in_proj",
    )(x, g, sc, sh, w, wdt)


def _ssd_pool_kernel(xbc_ref, xbch_ref, z0_ref, z1_ref, z2_ref, u_ref, uh_ref, dt_ref,
                     cw_ref, cb_ref, dtb_ref, alog_ref, dskip_ref, ng_ref, eh_ref,
                     wp_ref, bp_ref, ps_ref, y_ref,
                     st_scr, xe_scr, xa_scr, ue_scr, yd_scr, *, d_ssd, n_heads):
    i = pl.program_id(0)
    L = CHUNK
    gw = d_ssd // SSD_GROUPS
    halo_c = xe_scr.shape[0] - L
    halo_u = ue_scr.shape[0] - L
    first = i == 0

    @pl.when(first)
    def _():
        st_scr[...] = jnp.zeros_like(st_scr)

    xe_scr[0:halo_c, :] = jnp.where(first, 0.0, xbch_ref[...])
    xe_scr[halo_c:, :] = xbc_ref[...]
    ncols = xe_scr.shape[1]
    cblk = 512
    for c0 in range(0, ncols, cblk):
        xe = xe_scr[:, c0:c0 + cblk]
        xe2 = pltpu.roll(xe, 2, axis=0)
        w = [cw_ref[k:k + 1, c0:c0 + cblk] for k in range(CONV_K)]
        even = cb_ref[:, c0:c0 + cblk] + w[3] * xe[halo_c:, :] + w[1] * xe2[halo_c:, :]
        odd = pltpu.roll(w[2] * xe + w[0] * xe2, 1, axis=0)[halo_c:, :]
        xa_scr[:, c0:c0 + cblk] = _silu(even + odd)

    lane = lax.broadcasted_iota(jnp.int32, (L, LANES), 1)
    row = lax.broadcasted_iota(jnp.int32, (L, LANES), 0)
    head_ok = lane < n_heads
    dt = jnp.where(head_ok, jax.nn.softplus(dt_ref[...] + dtb_ref[...]), 0.0)
    a = -jnp.exp(alog_ref[...])
    da = dt * a
    causal = row >= lane
    tri = jnp.where(causal, 1.0, 0.0).astype(F32)
    cum = jnp.dot(tri, da, preferred_element_type=F32, precision=HIGHEST)
    cum_t = cum.T
    dt_t = dt.T
    cum_last = cum[L - 1:L, :]
    ecum = jnp.exp(cum)
    wend = jnp.exp(cum_last - cum) * dt
    eh = eh_ref[...]
    ecum_x = jnp.dot(ecum.astype(BF16), eh, preferred_element_type=F32)
    wend_x = jnp.dot(wend.astype(BF16), eh, preferred_element_type=F32)

    hpg = n_heads // SSD_GROUPS
    for g in range(SSD_GROUPS):
        b_g = xa_scr[:, d_ssd + g * D_STATE: d_ssd + (g + 1) * D_STATE]
        c_g = xa_scr[:, d_ssd + SSD_GROUPS * D_STATE + g * D_STATE:
                     d_ssd + SSD_GROUPS * D_STATE + (g + 1) * D_STATE]
        b_bf = b_g.astype(BF16)
        c_bf = c_g.astype(BF16)
        cb = lax.dot_general(c_bf, b_bf, (((1,), (1,)), ((), ())), preferred_element_type=F32)
        for hp in range(hpg // 2):
            h0 = g * hpg + 2 * hp
            ms = []
            for h in (h0, h0 + 1):
                seg = cum[:, h:h + 1] - cum_t[h:h + 1, :]
                decay = jnp.exp(jnp.where(causal, seg, -jnp.inf))
                ms.append(cb * decay * dt_t[h:h + 1, :])
            lhs = jnp.concatenate(ms, axis=1).astype(BF16)
            c0 = h0 * SSD_HEAD_DIM
            xpair = xa_scr[:, c0:c0 + LANES]
            top = jnp.where(lane < SSD_HEAD_DIM, xpair, 0.0)
            bot = jnp.where(lane >= SSD_HEAD_DIM, xpair, 0.0)
            rhs = jnp.concatenate([top, bot], axis=0).astype(BF16)
            yd_scr[:, c0:c0 + LANES] = jnp.dot(lhs, rhs, preferred_element_type=F32)
        gs = slice(g * gw, (g + 1) * gw)
        st = st_scr[g]
        y_off = jnp.dot(c_bf, st.astype(BF16), preferred_element_type=F32) * ecum_x[:, gs]
        xs_g = xa_scr[:, gs]
        y_g = yd_scr[:, gs] + y_off + dskip_ref[:, gs] * xs_g
        xw = (xs_g * wend_x[:, gs]).astype(BF16)
        st_scr[g] = st * ecum_x[L - 1:L, gs] + jnp.dot(b_g.T.astype(BF16), xw, preferred_element_type=F32)
        zparts = (z0_ref, z1_ref, z2_ref)
        zw = z0_ref.shape[1]
        zg = jnp.concatenate(
            [zparts[(g * gw + o) // zw][:, (g * gw + o) % zw:(g * gw + o) % zw + LANES] for o in range(0, gw, LANES)],
            axis=1)
        y_g = y_g * _silu(zg)
        msq = jnp.mean(y_g * y_g, axis=-1, keepdims=True)
        y_ref[:, gs] = (y_g * lax.rsqrt(msq + NORM_EPS) * ng_ref[:, gs]).astype(y_ref.dtype)

    ue_scr[0:halo_u, :] = jnp.where(first, 0.0, uh_ref[...])
    ue_scr[halo_u:, :] = u_ref[...]
    pg = ue_scr.shape[1] // len(POOL_WINDOWS)
    tpos = (i * L + lax.broadcasted_iota(jnp.int32, (L, pg), 0) + 1).astype(F32)
    for gi, w in enumerate(POOL_WINDOWS):
        cs = slice(gi * pg, (gi + 1) * pg)
        ue = ue_scr[:, cs]
        win = ue
        n = 1
        while n < w:
            win = win + pltpu.roll(win, n, axis=0)
            n *= 2
        win = win[halo_u:, :]
        tok = ue[halo_u:, :]
        pooled = win / jnp.minimum(tpos, float(w)) - tok
        yp = jnp.dot(pooled.astype(BF16), wp_ref[gi], preferred_element_type=F32)
        y_ref[:, d_ssd + gi * pg: d_ssd + (gi + 1) * pg] = ((yp + bp_ref[:, cs]) * ps_ref[:, cs]).astype(y_ref.dtype)


def _ssd_pool(proj, dt, cw, cb, dtb, alog, dskip_x, ng, eh, wp, bp, ps, d_ssd, d_conv, d_pool, n_heads):
    t = proj.shape[0]
    L = CHUNK
    halo_c, halo_u = 8, 16
    assert CONV_K == 4 and CONV_K - 1 <= halo_c
    assert all(w & (w - 1) == 0 and w <= halo_u for w in POOL_WINDOWS)
    zw = 1024
    zb = d_conv // zw
    ub = (d_conv + d_ssd) // d_pool
    d_mix = d_ssd + d_pool
    gw = d_ssd // SSD_GROUPS

    def full(shape):
        return pl.BlockSpec(shape, lambda i: (0,) * len(shape))

    in_specs = [
        pl.BlockSpec((L, d_conv), lambda i: (i, 0)),
        pl.BlockSpec((halo_c, d_conv), lambda i: (jnp.maximum(i * (L // halo_c) - 1, 0), 0)),
        pl.BlockSpec((L, zw), lambda i: (i, zb)),
        pl.BlockSpec((L, zw), lambda i: (i, zb + 1)),
        pl.BlockSpec((L, zw), lambda i: (i, zb + 2)),
        pl.BlockSpec((L, d_pool), lambda i: (i, ub)),
        pl.BlockSpec((halo_u, d_pool), lambda i: (jnp.maximum(i * (L // halo_u) - 1, 0), ub)),
        pl.BlockSpec((L, LANES), lambda i: (i, 0)),
        full(cw.shape), full(cb.shape), full(dtb.shape), full(alog.shape), full(dskip_x.shape),
        full(ng.shape), full(eh.shape), full(wp.shape), full(bp.shape), full(ps.shape),
    ]
    kern = functools.partial(_ssd_pool_kernel, d_ssd=d_ssd, n_heads=n_heads)
    return pl.pallas_call(
        kern,
        grid=(t // L,),
        in_specs=in_specs,
        out_specs=pl.BlockSpec((L, d_mix), lambda i: (i, 0)),
        out_shape=jax.ShapeDtypeStruct((t, d_mix), BF16),
        scratch_shapes=[pltpu.VMEM((SSD_GROUPS, D_STATE, gw), F32),
                        pltpu.VMEM((L + halo_c, d_conv), F32),
                        pltpu.VMEM((L, d_conv), F32),
                        pltpu.VMEM((L + halo_u, d_pool), F32),
                        pltpu.VMEM((L, d_ssd), F32)],
        compiler_params=_cparams(("arbitrary",)),
        name="ssd_pool",
    )(proj, proj, proj, proj, proj, proj, proj, dt, cw, cb, dtb, alog, dskip_x, ng, eh, wp, bp, ps)


def _pack_bf16_pairs(lo, hi):
    lo_b = pltpu.bitcast(lo.astype(BF16).astype(F32), jnp.uint32) >> 16
    hi_b = pltpu.bitcast(hi.astype(BF16).astype(F32), jnp.uint32) & jnp.uint32(0xFFFF0000)
    return lo_b | hi_b


def _unpack_bf16_pairs(w):
    lo = pltpu.bitcast(w << 16, F32).astype(BF16)
    hi = pltpu.bitcast(w & jnp.uint32(0xFFFF0000), F32).astype(BF16)
    return lo, hi


def _outproj_kernel(y_ref, x_ref, w_ref, g1_ref, ng_ref, sc_ref, sh_ref, wr_ref, br_ref,
                    x1_ref, h2p_ref, idx_ref, gate_ref, *, n_experts):
    mix = jnp.dot(y_ref[...], w_ref[...], preferred_element_type=F32)
    x1 = x_ref[...] + g1_ref[...] * mix
    x1_ref[...] = x1
    ms = jnp.mean(x1 * x1, axis=-1, keepdims=True)
    h = x1 * lax.rsqrt(ms + NORM_EPS) * ng_ref[...]
    h = h * (1.0 + sc_ref[...]) + sh_ref[...]
    half = h.shape[1] // 2
    h2p_ref[...] = _pack_bf16_pairs(h[:, :half], h[:, half:])

    h_hi = h.astype(BF16)
    h_mid = (h - h_hi.astype(F32)).astype(BF16)
    wr = wr_ref[...]
    r_hi = jnp.dot(h_hi, wr, preferred_element_type=F32)
    r_mid = jnp.dot(h_mid, wr, preferred_element_type=F32)
    logits = (r_hi + r_mid) + pltpu.roll(r_hi, LANES - n_experts, axis=1) + br_ref[...]
    tm = logits.shape[0]
    lane = lax.broadcasted_iota(jnp.int32, (tm, LANES), 1)
    lane_f = lane.astype(F32)
    vals = jnp.where(lane < n_experts, logits, -jnp.inf)
    top_v, top_i = [], []
    for _ in range(TOP_K):
        m = jnp.max(vals, axis=-1, keepdims=True)
        am = jnp.min(jnp.where(vals == m, lane_f, float(LANES)), axis=-1, keepdims=True)
        top_v.append(m)
        top_i.append(am)
        vals = jnp.where(lane_f == am, -jnp.inf, vals)
    es = [jnp.exp(v - top_v[0]) for v in top_v]
    denom = es[0] + es[1] + es[2] + es[3]
    idx_out = jnp.zeros((tm, LANES), F32)
    gate_out = jnp.zeros((tm, LANES), F32)
    for k in range(TOP_K):
        idx_out = jnp.where(lane == k, top_i[k], idx_out)
        gate_out = jnp.where(lane == k, es[k] / denom, gate_out)
    idx_ref[...] = idx_out.astype(jnp.int32)
    gate_ref[...] = gate_out


def _out_proj(ycat, x, w, g1, ng, sc, sh, wr, br, n_experts, tm):
    t, d = x.shape
    dm = ycat.shape[1]
    vec = pl.BlockSpec((1, d), lambda i: (0, 0))
    kern = functools.partial(_outproj_kernel, n_experts=n_experts)
    return pl.pallas_call(
        kern,
        grid=(t // tm,),
        in_specs=[pl.BlockSpec((tm, dm), lambda i: (i, 0)),
                  pl.BlockSpec((tm, d), lambda i: (i, 0)),
                  pl.BlockSpec((dm, d), lambda i: (0, 0), pipeline_mode=pl.Buffered(1)),
                  vec, vec, vec, vec,
                  pl.BlockSpec((d, LANES), lambda i: (0, 0)),
                  pl.BlockSpec((1, LANES), lambda i: (0, 0))],
        out_specs=[pl.BlockSpec((tm, d), lambda i: (i, 0)),
                   pl.BlockSpec((tm, d // 2), lambda i: (i, 0)),
                   pl.BlockSpec((tm, LANES), lambda i: (i, 0)),
                   pl.BlockSpec((tm, LANES), lambda i: (i, 0))],
        out_shape=[jax.ShapeDtypeStruct((t, d), F32),
                   jax.ShapeDtypeStruct((t, d // 2), jnp.uint32),
                   jax.ShapeDtypeStruct((t, LANES), jnp.int32),
                   jax.ShapeDtypeStruct((t, LANES), F32)],
        compiler_params=_cparams(("arbitrary",)),
        name="out_proj",
    )(ycat, x, w, g1, ng, sc, sh, wr, br)


def _first_half(x, c, w_ada, b_ada, norm1_g, w_in_proj, conv_w, conv_b, dt_bias, a_log, d_skip, ssd_norm_g,
                w_pool, b_pool, pool_scale, w_out_proj, norm2_g, w_router, b_router):
    _, t, d = x.shape
    n_heads = dt_bias.shape[1]
    d_ssd = n_heads * SSD_HEAD_DIM
    d_conv = conv_w.shape[2]
    d_pool = b_pool.shape[1]
    n_experts = w_router.shape[2]
    x2 = x.reshape(t, d)

    mod = _ada(c, w_ada, b_ada)
    sh1, sc1, g1, sh2, sc2, g2 = [mod[:, k * d:(k + 1) * d] for k in range(6)]

    w_main, w_dt = _prep_w_in(w_in_proj, d_ssd, d_conv, n_heads, d_pool)
    tm1 = min(1024, t)
    proj, dt_raw = _in_proj(x2, norm1_g, sc1, sh1, w_main, w_dt, tm1, 1024, n_heads)

    pad_h = LANES - n_heads
    dtb = jnp.pad(dt_bias, ((0, 0), (0, pad_h)))
    alog = jnp.pad(a_log, ((0, 0), (0, pad_h)))
    dskip_x = jnp.repeat(d_skip, SSD_HEAD_DIM, axis=1)
    eh = (jnp.arange(LANES)[:, None] == (jnp.arange(d_ssd) // SSD_HEAD_DIM)[None, :]).astype(BF16)
    ycat = _ssd_pool(proj, dt_raw, conv_w[0], conv_b, dtb, alog, dskip_x, ssd_norm_g, eh,
                     w_pool[0].astype(BF16), b_pool, pool_scale, d_ssd, d_conv, d_pool, n_heads)

    wr_hi = w_router[0].astype(BF16)
    wr_mid = (w_router[0] - wr_hi.astype(F32)).astype(BF16)
    assert 2 * n_experts <= LANES
    wr = jnp.pad(jnp.concatenate([wr_hi, wr_mid], axis=1), ((0, 0), (0, LANES - 2 * n_experts)))
    br = jnp.pad(b_router, ((0, 0), (0, LANES - n_experts)))
    x1, h2p, idx, gates = _out_proj(ycat, x2, w_out_proj[0].astype(BF16), g1, norm2_g, sc2, sh2, wr, br,
                                    n_experts, min(512, t))
    return x1, h2p, idx, gates, g2


MOE_SUB = 256
MOE_ROWS_MAX = 2304
MOE_TF = 256
MOE_TAB_ALIGN = 1024
MOE_TAB = 4096


def _route_tables(idx, n_experts):
    t, k = idx.shape
    rmax, sub = MOE_ROWS_MAX, MOE_SUB
    ns_max = n_experts + (t * k + rmax - 1) // rmax
    tok = jnp.arange(t, dtype=jnp.int32)[:, None]
    key = (idx * t + tok) * k + jnp.arange(k, dtype=jnp.int32)[None, :]
    skey = jnp.sort(key.reshape(-1))
    row_tok = (skey // k) % t
    row_dst = (skey % k) * t + row_tok
    bounds = jnp.searchsorted(skey, jnp.arange(n_experts + 1, dtype=jnp.int32) * (t * k)).astype(jnp.int32)
    off, counts = bounds[:-1], bounds[1:] - bounds[:-1]
    n_sup = (counts + rmax - 1) // rmax
    per = (counts + jnp.maximum(n_sup, 1) - 1) // jnp.maximum(n_sup, 1)
    rps = jnp.maximum((per + sub - 1) // sub * sub, sub)
    sup_end = jnp.cumsum(n_sup)
    sup_base = sup_end - n_sup
    s_ids = jnp.arange(ns_max + 1, dtype=jnp.int32)
    live = s_ids < sup_end[-1]
    st_e = jnp.minimum(jnp.searchsorted(sup_end, s_ids, side="right"), n_experts - 1).astype(jnp.int32)
    s_loc = s_ids - sup_base[st_e]
    st_n = jnp.where(live, jnp.clip(counts[st_e] - s_loc * rps[st_e], 0, rps[st_e]), 0).astype(jnp.int32)
    st_start = jnp.where(live, off[st_e] + s_loc * rps[st_e], 0).astype(jnp.int32)
    n_super = sup_end[-1:].astype(jnp.int32)
    row_tok = jnp.pad(row_tok, (0, MOE_TAB))
    row_dst = jnp.pad(row_dst, (0, MOE_TAB))
    return row_tok, row_dst, st_e, st_n, st_start, n_super


def _moe_kernel(st_e, st_n, st_start, nsup, tok_hbm, dst_hbm, h2p_hbm, win_hbm, wout_hbm, bin_ref, bout_ref,
                perm_ref, y_hbm, tok_s, dst_s, xbuf, acc, ybuf, win_st, wout_st, wi_bf0, wo_bf0, wi_bf1, wo_bf1,
                act_scr,
                tsem, gsem, ssem, wsem, *, nf):
    sub, rmax, tf = MOE_SUB, MOE_ROWS_MAX, MOE_TF
    nsub_max = rmax // sub
    gsub = sub // SUBLANES
    dh = xbuf.shape[2]
    d_out = acc.shape[2]
    n_super = nsup[0]

    def n_sub_of(s):
        return (st_n[s] + sub - 1) // sub

    def tab_base(s):
        return pl.multiple_of((st_start[s] // MOE_TAB_ALIGN) * MOE_TAB_ALIGN, MOE_TAB_ALIGN)

    def tab_off(s, slot):
        return slot * MOE_TAB + st_start[s] - tab_base(s)

    def table_copies(s, slot):
        win = pl.ds(tab_base(s), MOE_TAB)
        dst = pl.ds(pl.multiple_of(slot * MOE_TAB, MOE_TAB), MOE_TAB)
        return (pltpu.make_async_copy(tok_hbm.at[win], tok_s.at[dst], tsem.at[slot]),
                pltpu.make_async_copy(dst_hbm.at[win], dst_s.at[dst], tsem.at[slot]))

    def groups(m):
        return pl.ds(pl.multiple_of(m * gsub, gsub), gsub)

    def weight_copies(e, j):
        cols = pl.ds(pl.multiple_of(j * 2 * tf, 2 * tf), 2 * tf)
        hk = win_st.shape[0] // 2
        c_in0 = pltpu.make_async_copy(win_hbm.at[e, 0:hk, cols], win_st.at[0:hk], wsem.at[0])
        c_in1 = pltpu.make_async_copy(win_hbm.at[e, hk:2 * hk, cols], win_st.at[hk:2 * hk], wsem.at[0])
        c_out = pltpu.make_async_copy(wout_hbm.at[e, pl.ds(pl.multiple_of(j * tf, tf), tf), :],
                                      wout_st, wsem.at[0])
        return (c_in0, 0), (c_in1, 1), (c_out, 0)

    def gather_rows(m, toff):
        def body(i, carry):
            g = m * gsub + i
            for u in range(SUBLANES):
                tk = tok_s[toff + g * SUBLANES + u]
                pltpu.make_async_copy(h2p_hbm.at[pl.ds(tk, 1)], xbuf.at[g, pl.ds(u, 1)],
                                      gsem.at[m]).start(priority=u % 2)
            return carry
        lax.fori_loop(0, gsub, body, 0)

    def wait_gather(m):
        pltpu.make_async_copy(xbuf.at[groups(m)], xbuf.at[groups(m)], gsem.at[m]).wait()

    def scatter_rows(m, toff, nvalid):
        def body(i, carry):
            g = m * gsub + i
            for u in range(SUBLANES):
                d = dst_s[toff + g * SUBLANES + u]
                pltpu.make_async_copy(ybuf.at[g, pl.ds(u, 1)], y_hbm.at[pl.ds(d, 1)],
                                      ssem.at[m]).start(priority=u % 2)
            return carry
        n_full = nvalid // SUBLANES
        lax.fori_loop(0, n_full, body, 0)
        g_last = m * gsub + n_full

        def tail(u, carry):
            d = dst_s[toff + g_last * SUBLANES + u]
            pltpu.make_async_copy(ybuf.at[g_last, pl.ds(u, 1)], y_hbm.at[pl.ds(d, 1)], ssem.at[m]).start()
            return carry
        lax.fori_loop(0, nvalid - n_full * SUBLANES, tail, 0)

    def wait_scatter(m, nvalid):
        @pl.when(nvalid == sub)
        def _():
            pltpu.make_async_copy(ybuf.at[groups(m)], ybuf.at[groups(m)], ssem.at[m]).wait()

        @pl.when(nvalid < sub)
        def _():
            def body(i, carry):
                pltpu.make_async_copy(ybuf.at[0, pl.ds(0, 1)], y_hbm.at[pl.ds(0, 1)], ssem.at[m]).wait()
                return carry
            lax.fori_loop(0, nvalid, body, 0)

    def valid_rows(n_rows, m):
        return jnp.clip(n_rows - m * sub, 0, sub)

    xbuf[...] = jnp.zeros_like(xbuf)
    acc[...] = jnp.zeros_like(acc)
    for cp in table_copies(0, 0):
        cp.start()
    for cp in table_copies(0, 0):
        cp.wait()

    act_a, act_b = act_scr.at[0], act_scr.at[1]

    def succ(s, j):
        last = j == nf - 1
        s1 = jnp.where(last, jnp.minimum(s + 1, n_super - 1), s)
        j1 = jnp.where(last, jnp.where(s + 1 < n_super, 0, j), j + 1)
        return s1, j1

    wbufs = ((wi_bf0, wo_bf0), (wi_bf1, wo_bf1))

    def convert(ws):
        wi, wo = wbufs[ws]
        perm = perm_ref[...]
        rblk = 512
        for cgrp in range(2 * tf // 256):
            for rb in range(0, wi.shape[0], rblk):
                wblk = win_st[rb:rb + rblk, cgrp * 256:(cgrp + 1) * 256].astype(BF16)
                pw = jnp.dot(wblk, perm, preferred_element_type=F32).astype(BF16)
                wi[rb:rb + rblk, cgrp * LANES:(cgrp + 1) * LANES] = pw[:, :LANES]
                wi[rb:rb + rblk, tf + cgrp * LANES: tf + (cgrp + 1) * LANES] = pw[:, LANES:]
        wo[...] = wout_st[...].astype(BF16)

    def stage1(m, act_ref, ws, b_in):
        wi = wbufs[ws][0]
        x_lo, x_hi = _unpack_bf16_pairs(xbuf[groups(m)].reshape(sub, dh))
        hb = (jnp.dot(x_lo, wi[0:dh, :], preferred_element_type=F32)
              + jnp.dot(x_hi, wi[dh:2 * dh, :], preferred_element_type=F32) + b_in)
        glu = jnp.minimum(hb[:, :tf], SWIGLU_LIMIT)
        lin = jnp.clip(hb[:, tf:], -SWIGLU_LIMIT, SWIGLU_LIMIT)
        act = glu * jax.nn.sigmoid(SWIGLU_ALPHA * glu) * (lin + 1.0)
        act_ref[...] = act.astype(BF16)

    def stage2(m, act_ref, ws, j):
        o = jnp.dot(act_ref[...], wbufs[ws][1][...], preferred_element_type=F32)
        prev = acc[groups(m)].reshape(sub, d_out)
        acc[groups(m)] = (jnp.where(j > 0, prev, 0.0) + o).reshape(gsub, SUBLANES, d_out)

    for cp, prio in weight_copies(st_e[0], 0):
        cp.start(priority=prio)
    toff0 = tab_off(0, 0)
    lax.fori_loop(0, n_sub_of(0), lambda m, c: (gather_rows(m, toff0), c)[1], 0)
    for cp, _ in weight_copies(st_e[0], 0):
        cp.wait()
    convert(0)
    s_nx, j_nx = succ(0, 0)
    for cp, prio in weight_copies(st_e[s_nx], j_nx):
        cp.start(priority=prio)

    def supertile(s, carry):
        e = st_e[s]
        n_rows = st_n[s]
        n_sub = n_sub_of(s)
        tslot = s % 2
        toff = tab_off(s, tslot)
        n_rows_prev = jnp.where(s > 0, st_n[jnp.maximum(s - 1, 0)], 0)
        for cp in table_copies(s + 1, 1 - tslot):
            cp.start()
        lax.fori_loop(0, n_sub, lambda m, c2: (wait_gather(m), c2)[1], 0)
        stage1(0, act_a, 0, bin_ref[e, pl.ds(0, 1), :])

        def chunk(j, ws):
            b_in = bin_ref[e, pl.ds(j, 1), :]
            n_pairs = (n_sub - 1) // 2

            def pair(p, carry):
                m = 2 * p
                stage1(m + 1, act_b, ws, b_in)
                stage2(m, act_a, ws, j)
                stage1(m + 2, act_a, ws, b_in)
                stage2(m + 1, act_b, ws, j)
                return carry
            lax.fori_loop(0, n_pairs, pair, 0)
            m_last = 2 * n_pairs

            s1, j1 = succ(s, j)
            e1 = st_e[s1]
            for cp, _ in weight_copies(e1, j1):
                cp.wait()
            b_in1 = bin_ref[e1, pl.ds(j1, 1), :]

            @pl.when(m_last < n_sub - 1)
            def _():
                stage1(m_last + 1, act_b, ws, b_in)
                stage2(m_last, act_a, ws, j)
                stage2(m_last + 1, act_b, ws, j)
                convert(1 - ws)
                stage1(0, act_a, 1 - ws, b_in1)

            @pl.when(m_last == n_sub - 1)
            def _():
                stage2(m_last, act_a, ws, j)
                convert(1 - ws)
                stage1(0, act_a, 1 - ws, b_in1)

            s2, j2 = succ(s1, j1)
            for cp, prio in weight_copies(st_e[s2], j2):
                cp.start(priority=prio)

        def chunk_pair(jj, carry):
            chunk(2 * jj, 0)
            chunk(2 * jj + 1, 1)
            return carry
        lax.fori_loop(0, nf // 2, chunk_pair, 0)

        for cp in table_copies(s + 1, 1 - tslot):
            cp.wait()
        toff_next = tab_off(s + 1, 1 - tslot)
        lax.fori_loop(0, n_sub_of(s + 1), lambda m, c: (gather_rows(m, toff_next), c)[1], 0)
        n_sub_prev = (n_rows_prev + sub - 1) // sub
        lax.fori_loop(n_sub, jnp.maximum(n_sub, n_sub_prev),
                      lambda m, c: (wait_scatter(m, valid_rows(n_rows_prev, m)), c)[1], 0)
        b_out = bout_ref[pl.ds(e, 1), :]

        def finish(m, carry):
            wait_scatter(m, valid_rows(n_rows_prev, m))
            v = acc[groups(m)].reshape(sub, d_out) + b_out
            ybuf[groups(m)] = _pack_bf16_pairs(v[:, :dh], v[:, dh:]).reshape(gsub, SUBLANES, dh)
            scatter_rows(m, toff, valid_rows(n_rows, m))
            return carry
        lax.fori_loop(0, n_sub, finish, 0)
        return carry
    lax.fori_loop(0, n_super, supertile, 0)

    for cp, _ in weight_copies(st_e[0], 0):
        cp.wait()
    n_rows_last = jnp.where(n_super > 0, st_n[jnp.maximum(n_super - 1, 0)], 0)
    for m in range(nsub_max):
        wait_scatter(m, valid_rows(n_rows_last, m))


def _moe(h2p, row_tok, row_dst, st_e, st_n, st_start, n_super, w_in, w_out, b_in_c, b_out, perm, t):
    n_experts, d, f2 = w_in.shape
    nf = f2 // (2 * MOE_TF)
    rmax = MOE_ROWS_MAX
    nsub_max = rmax // MOE_SUB
    assert rmax % MOE_SUB == 0 and MOE_SUB % SUBLANES == 0 and nf % 2 == 0
    assert MOE_TAB >= rmax + MOE_TAB_ALIGN and MOE_TAB % MOE_TAB_ALIGN == 0
    smem = pl.BlockSpec(memory_space=pltpu.SMEM)
    hbm = pl.BlockSpec(memory_space=pl.ANY)
    vmem = pl.BlockSpec(memory_space=pltpu.VMEM)
    kern = functools.partial(_moe_kernel, nf=nf)
    return pl.pallas_call(
        kern,
        in_specs=[smem, smem, smem, smem, hbm, hbm, hbm, hbm, hbm, vmem, vmem, vmem],
        out_specs=hbm,
        out_shape=jax.ShapeDtypeStruct((TOP_K * t, d // 2), jnp.uint32),
        scratch_shapes=[pltpu.SMEM((2 * MOE_TAB,), jnp.int32),
                        pltpu.SMEM((2 * MOE_TAB,), jnp.int32),
                        pltpu.VMEM((rmax // SUBLANES, SUBLANES, d // 2), jnp.uint32),
                        pltpu.VMEM((rmax // SUBLANES, SUBLANES, d), F32),
                        pltpu.VMEM((rmax // SUBLANES, SUBLANES, d // 2), jnp.uint32),
                        pltpu.VMEM((d, 2 * MOE_TF), F32),
                        pltpu.VMEM((MOE_TF, d), F32),
                        pltpu.VMEM((d, 2 * MOE_TF), BF16),
                        pltpu.VMEM((MOE_TF, d), BF16),
                        pltpu.VMEM((d, 2 * MOE_TF), BF16),
                        pltpu.VMEM((MOE_TF, d), BF16),
                        pltpu.VMEM((2, MOE_SUB, MOE_TF), BF16),
                        pltpu.SemaphoreType.DMA((2,)),
                        pltpu.SemaphoreType.DMA((nsub_max,)),
                        pltpu.SemaphoreType.DMA((nsub_max,)),
                        pltpu.SemaphoreType.DMA((1,))],
        compiler_params=pltpu.CompilerParams(vmem_limit_bytes=MOE_VMEM_LIMIT, has_side_effects=True),
        name="moe",
    )(st_e, st_n, st_start, n_super, row_tok, row_dst, h2p, w_in, w_out, b_in_c, b_out, perm)


def _combine_kernel(y0_ref, y1_ref, y2_ref, y3_ref, gate_ref, x1_ref, g2_ref, fg_ref, o_ref):
    g = gate_ref[...]
    half = y0_ref.shape[1]
    y_lo = jnp.zeros(y0_ref.shape, F32)
    y_hi = jnp.zeros(y0_ref.shape, F32)
    for k, y_ref in enumerate((y0_ref, y1_ref, y2_ref, y3_ref)):
        w = y_ref[...]
        y_lo = y_lo + pltpu.bitcast(w << 16, F32) * g[:, k:k + 1]
        y_hi = y_hi + pltpu.bitcast(w & jnp.uint32(0xFFFF0000), F32) * g[:, k:k + 1]
    x_lo = x1_ref[:, :half] + g2_ref[:, :half] * y_lo
    x_hi = x1_ref[:, half:] + g2_ref[:, half:] * y_hi
    ssq = jnp.sum(x_lo * x_lo, axis=-1, keepdims=True) + jnp.sum(x_hi * x_hi, axis=-1, keepdims=True)
    inv = lax.rsqrt(ssq / (2 * half) + NORM_EPS)
    o_ref[:, :half] = x_lo * inv * fg_ref[:, :half]
    o_ref[:, half:] = x_hi * inv * fg_ref[:, half:]


def _combine(y, gates, x1, g2, fg, tm):
    t, d = x1.shape
    nb = t // tm
    vec = pl.BlockSpec((1, d), lambda i: (0, 0))
    yspecs = [pl.BlockSpec((tm, d // 2), functools.partial(lambda i, k: (k * nb + i, 0), k=k))
              for k in range(TOP_K)]
    return pl.pallas_call(
        _combine_kernel,
        grid=(nb,),
        in_specs=yspecs + [pl.BlockSpec((tm, LANES), lambda i: (i, 0)),
                           pl.BlockSpec((tm, d), lambda i: (i, 0)), vec, vec],
        out_specs=pl.BlockSpec((tm, d), lambda i: (i, 0)),
        out_shape=jax.ShapeDtypeStruct((t, d), F32),
        compiler_params=_cparams(("arbitrary",)),
        name="combine",
    )(y, y, y, y, gates, x1, g2, fg)


def kernel(x, c, w_ada, b_ada, norm1_g, w_in_proj, conv_w, conv_b, dt_bias, a_log, d_skip, ssd_norm_g, w_pool,
           b_pool, pool_scale, w_out_proj, norm2_g, w_router, b_router, w_exp_in, b_exp_in, w_exp_out, b_exp_out,
           final_norm_g):
    assert x.shape[0] == 1 and w_ada.shape[0] == 1
    _, t, d = x.shape
    n_experts = w_router.shape[2]
    x1, h2p, idx, gates, g2 = _first_half(x, c, w_ada, b_ada, norm1_g, w_in_proj, conv_w, conv_b, dt_bias, a_log,
                                          d_skip, ssd_norm_g, w_pool, b_pool, pool_scale, w_out_proj, norm2_g,
                                          w_router, b_router)
    row_tok, row_dst, st_e, st_n, st_start, n_super = _route_tables(idx[:, :TOP_K], n_experts)
    f = w_exp_out.shape[2]
    nf = f // MOE_TF
    b_in = b_exp_in[0].reshape(n_experts, nf, MOE_TF, 2)
    b_in_c = jnp.concatenate([b_in[..., 0], b_in[..., 1]], axis=-1)
    src = jnp.concatenate([2 * jnp.arange(LANES), 2 * jnp.arange(LANES) + 1])
    perm = (jnp.arange(2 * LANES)[:, None] == src[None, :]).astype(BF16)
    y = _moe(h2p, row_tok, row_dst, st_e, st_n, st_start, n_super, w_exp_in.reshape(w_exp_in.shape[1:]),
             w_exp_out.reshape(w_exp_out.shape[1:]), b_in_c, b_exp_out.reshape(n_experts, d), perm, t)
    out = _combine(y, gates, x1, g2, final_norm_g.reshape(1, d), min(256, t))
    return out.reshape(x.shape)
```

```python
import functools

import jax
import jax.numpy as jnp
from jax import lax
from jax.experimental import pallas as pl
from jax.experimental.pallas import tpu as pltpu

F32 = jnp.float32
BF16 = jnp.bfloat16
HIGHEST = lax.Precision.HIGHEST

SSD_HEAD_DIM = 64
SSD_GROUPS = 4
D_STATE = 128
CONV_K = 4
CHUNK = 128
POOL_WINDOWS = (2, 4, 8, 16)
TOP_K = 4
SWIGLU_LIMIT = 7.0
SWIGLU_ALPHA = 1.702
NORM_EPS = 1e-6

LANES = 128
SUBLANES = 8
VMEM_LIMIT = 56 * 1024 * 1024
MOE_VMEM_LIMIT = 60 * 1024 * 1024


def _cparams(sem, vmem=VMEM_LIMIT):
    return pltpu.CompilerParams(dimension_semantics=sem, vmem_limit_bytes=vmem)


def _silu(v):
    return v * jax.nn.sigmoid(v)


def _ada_kernel(c_ref, w_ref, b_ref, o_ref):
    cond = _silu(c_ref[...])
    cond8 = jnp.broadcast_to(cond, (SUBLANES, cond.shape[1]))
    o = jnp.dot(cond8, w_ref[...], preferred_element_type=F32, precision=HIGHEST)
    o_ref[...] = o[0:1] + b_ref[...]


def _ada(c, w, b):
    _, d, n = w.shape
    tn = 1536
    return pl.pallas_call(
        _ada_kernel,
        grid=(n // tn,),
        in_specs=[pl.BlockSpec((1, d), lambda j: (0, 0)),
                  pl.BlockSpec((None, d, tn), lambda j: (0, 0, j)),
                  pl.BlockSpec((1, tn), lambda j: (0, j))],
        out_specs=pl.BlockSpec((1, tn), lambda j: (0, j)),
        out_shape=jax.ShapeDtypeStruct((1, n), F32),
        compiler_params=_cparams(("arbitrary",)),
        name="ada",
    )(c, w, b)


def _prep_w_kernel(a_ref, dt_rows_ref, wm_ref, wdt_ref, *, n_heads):
    wm_ref[...] = a_ref[...].T.astype(BF16)

    @pl.when(pl.program_id(0) == pl.num_programs(0) - 1)
    def _():
        t0 = dt_rows_ref[...].T
        lane = lax.broadcasted_iota(jnp.int32, t0.shape, 1)
        hi = t0.astype(BF16).astype(F32)
        mid = pltpu.roll(t0 - hi, n_heads, axis=1)
        wdt_ref[...] = jnp.where(lane < n_heads, hi, jnp.where(lane < 2 * n_heads, mid, 0.0)).astype(BF16)


def _prep_w_in(w, d_ssd, d_conv, n_heads, d_pool):
    _, d, n_in = w.shape
    tn = d_pool
    assert d_ssd % tn == 0 and d_conv % tn == 0 and n_in == d_ssd + d_conv + n_heads + d_pool
    assert 2 * n_heads <= LANES and (d_ssd + d_conv) % LANES == 0 and n_heads % SUBLANES == 0
    nz, nx = d_ssd // tn, d_conv // tn
    n_main = nz + nx + 1
    wt = jnp.swapaxes(w, 1, 2).reshape(n_in, d)

    def row_off(j):
        t8 = tn // SUBLANES
        r8 = jnp.where(j < nx, d_ssd // SUBLANES + j * t8,
                       jnp.where(j < nx + nz, (j - nx) * t8, (d_ssd + d_conv + n_heads) // SUBLANES))
        return r8 * SUBLANES
    return pl.pallas_call(
        functools.partial(_prep_w_kernel, n_heads=n_heads),
        grid=(n_main,),
        in_specs=[pl.BlockSpec((pl.Element(tn), pl.Element(d)), lambda j: (row_off(j), 0)),
                  pl.BlockSpec((LANES, d), lambda j: ((d_ssd + d_conv) // LANES, 0))],
        out_specs=[pl.BlockSpec((d, tn), lambda j: (0, j)),
                   pl.BlockSpec((d, LANES), lambda j: (0, 0))],
        out_shape=[jax.ShapeDtypeStruct((d, n_main * tn), BF16), jax.ShapeDtypeStruct((d, LANES), BF16)],
        compiler_params=_cparams(("arbitrary",)),
        name="prep_w_in",
    )(wt, wt)


def _inproj_kernel(x_ref, g_ref, sc_ref, sh_ref, w_ref, wdt_ref, proj_ref, dt_ref, h_scr, *, n_heads):
    @pl.when(pl.program_id(1) == 0)
    def _():
        x = x_ref[...]
        ms = jnp.mean(x * x, axis=-1, keepdims=True)
        h = x * lax.rsqrt(ms + NORM_EPS) * g_ref[...]
        h = h * (1.0 + sc_ref[...]) + sh_ref[...]
        h_hi = h.astype(BF16)
        h_scr[...] = h_hi
        h_mid = (h - h_hi.astype(F32)).astype(BF16)
        wdt = wdt_ref[...]
        r_hi = jnp.dot(h_hi, wdt, preferred_element_type=F32)
        r_mid = jnp.dot(h_mid, wdt, preferred_element_type=F32)
        dt_ref[...] = (r_hi + r_mid) + pltpu.roll(r_hi, LANES - n_heads, axis=1)

    proj_ref[...] = jnp.dot(h_scr[...], w_ref[...], preferred_element_type=F32)


def _in_proj(x, g, sc, sh, w, wdt, tm, tn, n_heads):
    t, d = x.shape
    n = w.shape[1]
    vec = pl.BlockSpec((1, d), lambda i, j: (0, 0))
    return pl.pallas_call(
        functools.partial(_inproj_kernel, n_heads=n_heads),
        grid=(t // tm, n // tn),
        in_specs=[pl.BlockSpec((tm, d), lambda i, j: (i, 0)), vec, vec, vec,
                  pl.BlockSpec((d, tn), lambda i, j: (0, j)),
                  pl.BlockSpec((d, LANES), lambda i, j: (0, 0))],
        out_specs=[pl.BlockSpec((tm, tn), lambda i, j: (i, j)),
                   pl.BlockSpec((tm, LANES), lambda i, j: (i, 0))],
        out_shape=[jax.ShapeDtypeStruct((t, n), F32), jax.ShapeDtypeStruct((t, LANES), F32)],
        scratch_shapes=[pltpu.VMEM((tm, d), BF16)],
        compiler_params=_cparams(("arbitrary", "arbitrary")),
        name="in_proj",
    )(x, g, sc, sh, w, wdt)


def _ssd_pool_kernel(xbc_ref, xbch_ref, z0_ref, z1_ref, z2_ref, u_ref, uh_ref, dt_ref,
                     cw_ref, cb_ref, dtb_ref, alog_ref, dskip_ref, ng_ref, eh_ref,
                     wp_ref, bp_ref, ps_ref, y_ref,
                     st_scr, xe_scr, xa_scr, ue_scr, yd_scr, *, d_ssd, n_heads):
    i = pl.program_id(0)
    L = CHUNK
    gw = d_ssd // SSD_GROUPS
    halo_c = xe_scr.shape[0] - L
    halo_u = ue_scr.shape[0] - L
    first = i == 0

    @pl.when(first)
    def _():
        st_scr[...] = jnp.zeros_like(st_scr)

    xe_scr[0:halo_c, :] = jnp.where(first, 0.0, xbch_ref[...])
    xe_scr[halo_c:, :] = xbc_ref[...]
    ncols = xe_scr.shape[1]
    cblk = 512
    for c0 in range(0, ncols, cblk):
        xe = xe_scr[:, c0:c0 + cblk]
        xe2 = pltpu.roll(xe, 2, axis=0)
        w = [cw_ref[k:k + 1, c0:c0 + cblk] for k in range(CONV_K)]
        even = cb_ref[:, c0:c0 + cblk] + w[3] * xe[halo_c:, :] + w[1] * xe2[halo_c:, :]
        odd = pltpu.roll(w[2] * xe + w[0] * xe2, 1, axis=0)[halo_c:, :]
        xa_scr[:, c0:c0 + cblk] = _silu(even + odd)

    lane = lax.broadcasted_iota(jnp.int32, (L, LANES), 1)
    row = lax.broadcasted_iota(jnp.int32, (L, LANES), 0)
    head_ok = lane < n_heads
    dt = jnp.where(head_ok, jax.nn.softplus(dt_ref[...] + dtb_ref[...]), 0.0)
    a = -jnp.exp(alog_ref[...])
    da = dt * a
    causal = row >= lane
    tri = jnp.where(causal, 1.0, 0.0).astype(F32)
    cum = jnp.dot(tri, da, preferred_element_type=F32, precision=HIGHEST)
    cum_t = cum.T
    dt_t = dt.T
    cum_last = cum[L - 1:L, :]
    ecum = jnp.exp(cum)
    wend = jnp.exp(cum_last - cum) * dt
    eh = eh_ref[...]
    ecum_x = jnp.dot(ecum.astype(BF16), eh, preferred_element_type=F32)
    wend_x = jnp.dot(wend.astype(BF16), eh, preferred_element_type=F32)

    hpg = n_heads // SSD_GROUPS
    for g in range(SSD_GROUPS):
        b_g = xa_scr[:, d_ssd + g * D_STATE: d_ssd + (g + 1) * D_STATE]
        c_g = xa_scr[:, d_ssd + SSD_GROUPS * D_STATE + g * D_STATE:
                     d_ssd + SSD_GROUPS * D_STATE + (g + 1) * D_STATE]
        b_bf = b_g.astype(BF16)
        c_bf = c_g.astype(BF16)
        cb = lax.dot_general(c_bf, b_bf, (((1,), (1,)), ((), ())), preferred_element_type=F32)
        for hp in range(hpg // 2):
            h0 = g * hpg + 2 * hp
            ms = []
            for h in (h0, h0 + 1):
                seg = cum[:, h:h + 1] - cum_t[h:h + 1, :]
                decay = jnp.exp(jnp.where(causal, seg, -jnp.inf))
                ms.append(cb * decay * dt_t[h:h + 1, :])
            lhs = jnp.concatenate(ms, axis=1).astype(BF16)
            c0 = h0 * SSD_HEAD_DIM
            xpair = xa_scr[:, c0:c0 + LANES]
            top = jnp.where(lane < SSD_HEAD_DIM, xpair, 0.0)
            bot = jnp.where(lane >= SSD_HEAD_DIM, xpair, 0.0)
            rhs = jnp.concatenate([top, bot], axis=0).astype(BF16)
            yd_scr[:, c0:c0 + LANES] = jnp.dot(lhs, rhs, preferred_element_type=F32)
        gs = slice(g * gw, (g + 1) * gw)
        st = st_scr[g]
        y_off = jnp.dot(c_bf, st.astype(BF16), preferred_element_type=F32) * ecum_x[:, gs]
        xs_g = xa_scr[:, gs]
        y_g = yd_scr[:, gs] + y_off + dskip_ref[:, gs] * xs_g
        xw = (xs_g * wend_x[:, gs]).astype(BF16)
        st_scr[g] = st * ecum_x[L - 1:L, gs] + jnp.dot(b_g.T.astype(BF16), xw, preferred_element_type=F32)
        zparts = (z0_ref, z1_ref, z2_ref)
        zw = z0_ref.shape[1]
        zg = jnp.concatenate(
            [zparts[(g * gw + o) // zw][:, (g * gw + o) % zw:(g * gw + o) % zw + LANES] for o in range(0, gw, LANES)],
            axis=1)
        y_g = y_g * _silu(zg)
        msq = jnp.mean(y_g * y_g, axis=-1, keepdims=True)
        y_ref[:, gs] = (y_g * lax.rsqrt(msq + NORM_EPS) * ng_ref[:, gs]).astype(y_ref.dtype)

    ue_scr[0:halo_u, :] = jnp.where(first, 0.0, uh_ref[...])
    ue_scr[halo_u:, :] = u_ref[...]
    pg = ue_scr.shape[1] // len(POOL_WINDOWS)
    tpos = (i * L + lax.broadcasted_iota(jnp.int32, (L, pg), 0) + 1).astype(F32)
    for gi, w in enumerate(POOL_WINDOWS):
        cs = slice(gi * pg, (gi + 1) * pg)
        ue = ue_scr[:, cs]
        win = ue
        n = 1
        while n < w:
            win = win + pltpu.roll(win, n, axis=0)
            n *= 2
        win = win[halo_u:, :]
        tok = ue[halo_u:, :]
        pooled = win / jnp.minimum(tpos, float(w)) - tok
        yp = jnp.dot(pooled.astype(BF16), wp_ref[gi], preferred_element_type=F32)
        y_ref[:, d_ssd + gi * pg: d_ssd + (gi + 1) * pg] = ((yp + bp_ref[:, cs]) * ps_ref[:, cs]).astype(y_ref.dtype)


def _ssd_pool(proj, dt, cw, cb, dtb, alog, dskip_x, ng, eh, wp, bp, ps, d_ssd, d_conv, d_pool, n_heads):
    t = proj.shape[0]
    L = CHUNK
    halo_c, halo_u = 8, 16
    assert CONV_K == 4 and CONV_K - 1 <= halo_c
    assert all(w & (w - 1) == 0 and w <= halo_u for w in POOL_WINDOWS)
    zw = 1024
    zb = d_conv // zw
    ub = (d_conv + d_ssd) // d_pool
    d_mix = d_ssd + d_pool
    gw = d_ssd // SSD_GROUPS

    def full(shape):
        return pl.BlockSpec(shape, lambda i: (0,) * len(shape))

    in_specs = [
        pl.BlockSpec((L, d_conv), lambda i: (i, 0)),
        pl.BlockSpec((halo_c, d_conv), lambda i: (jnp.maximum(i * (L // halo_c) - 1, 0), 0)),
        pl.BlockSpec((L, zw), lambda i: (i, zb)),
        pl.BlockSpec((L, zw), lambda i: (i, zb + 1)),
        pl.BlockSpec((L, zw), lambda i: (i, zb + 2)),
        pl.BlockSpec((L, d_pool), lambda i: (i, ub)),
        pl.BlockSpec((halo_u, d_pool), lambda i: (jnp.maximum(i * (L // halo_u) - 1, 0), ub)),
        pl.BlockSpec((L, LANES), lambda i: (i, 0)),
        full(cw.shape), full(cb.shape), full(dtb.shape), full(alog.shape), full(dskip_x.shape),
        full(ng.shape), full(eh.shape), full(wp.shape), full(bp.shape), full(ps.shape),
    ]
    kern = functools.partial(_ssd_pool_kernel, d_ssd=d_ssd, n_heads=n_heads)
    return pl.pallas_call(
        kern,
        grid=(t // L,),
        in_specs=in_specs,
        out_specs=pl.BlockSpec((L, d_mix), lambda i: (i, 0)),
        out_shape=jax.ShapeDtypeStruct((t, d_mix), BF16),
        scratch_shapes=[pltpu.VMEM((SSD_GROUPS, D_STATE, gw), F32),
                        pltpu.VMEM((L + halo_c, d_conv), F32),
                        pltpu.VMEM((L, d_conv), F32),
                        pltpu.VMEM((L + halo_u, d_pool), F32),
                        pltpu.VMEM((L, d_ssd), F32)],
        compiler_params=_cparams(("arbitrary",)),
        name="ssd_pool",
    )(proj, proj, proj, proj, proj, proj, proj, dt, cw, cb, dtb, alog, dskip_x, ng, eh, wp, bp, ps)


def _pack_bf16_pairs(lo, hi):
    lo_b = pltpu.bitcast(lo.astype(BF16).astype(F32), jnp.uint32) >> 16
    hi_b = pltpu.bitcast(hi.astype(BF16).astype(F32), jnp.uint32) & jnp.uint32(0xFFFF0000)
    return lo_b | hi_b


def _unpack_bf16_pairs(w):
    lo = pltpu.bitcast(w << 16, F32).astype(BF16)
    hi = pltpu.bitcast(w & jnp.uint32(0xFFFF0000), F32).astype(BF16)
    return lo, hi


def _outproj_kernel(y_ref, x_ref, w_ref, g1_ref, ng_ref, sc_ref, sh_ref, wr_ref, br_ref,
                    x1_ref, h2p_ref, idx_ref, gate_ref, *, n_experts):
    mix = jnp.dot(y_ref[...], w_ref[...], preferred_element_type=F32)
    x1 = x_ref[...] + g1_ref[...] * mix
    x1_ref[...] = x1
    ms = jnp.mean(x1 * x1, axis=-1, keepdims=True)
    h = x1 * lax.rsqrt(ms + NORM_EPS) * ng_ref[...]
    h = h * (1.0 + sc_ref[...]) + sh_ref[...]
    half = h.shape[1] // 2
    h2p_ref[...] = _pack_bf16_pairs(h[:, :half], h[:, half:])

    h_hi = h.astype(BF16)
    h_mid = (h - h_hi.astype(F32)).astype(BF16)
    wr = wr_ref[...]
    r_hi = jnp.dot(h_hi, wr, preferred_element_type=F32)
    r_mid = jnp.dot(h_mid, wr, preferred_element_type=F32)
    logits = (r_hi + r_mid) + pltpu.roll(r_hi, LANES - n_experts, axis=1) + br_ref[...]
    tm = logits.shape[0]
    lane = lax.broadcasted_iota(jnp.int32, (tm, LANES), 1)
    lane_f = lane.astype(F32)
    vals = jnp.where(lane < n_experts, logits, -jnp.inf)
    top_v, top_i = [], []
    for _ in range(TOP_K):
        m = jnp.max(vals, axis=-1, keepdims=True)
        am = jnp.min(jnp.where(vals == m, lane_f, float(LANES)), axis=-1, keepdims=True)
        top_v.append(m)
        top_i.append(am)
        vals = jnp.where(lane_f == am, -jnp.inf, vals)
    es = [jnp.exp(v - top_v[0]) for v in top_v]
    denom = es[0] + es[1] + es[2] + es[3]
    idx_out = jnp.zeros((tm, LANES), F32)
    gate_out = jnp.zeros((tm, LANES), F32)
    for k in range(TOP_K):
        idx_out = jnp.where(lane == k, top_i[k], idx_out)
        gate_out = jnp.where(lane == k, es[k] / denom, gate_out)
    idx_ref[...] = idx_out.astype(jnp.int32)
    gate_ref[...] = gate_out


def _out_proj(ycat, x, w, g1, ng, sc, sh, wr, br, n_experts, tm):
    t, d = x.shape
    dm = ycat.shape[1]
    vec = pl.BlockSpec((1, d), lambda i: (0, 0))
    kern = functools.partial(_outproj_kernel, n_experts=n_experts)
    return pl.pallas_call(
        kern,
        grid=(t // tm,),
        in_specs=[pl.BlockSpec((tm, dm), lambda i: (i, 0)),
                  pl.BlockSpec((tm, d), lambda i: (i, 0)),
                  pl.BlockSpec((dm, d), lambda i: (0, 0), pipeline_mode=pl.Buffered(1)),
                  vec, vec, vec, vec,
                  pl.BlockSpec((d, LANES), lambda i: (0, 0)),
                  pl.BlockSpec((1, LANES), lambda i: (0, 0))],
        out_specs=[pl.BlockSpec((tm, d), lambda i: (i, 0)),
                   pl.BlockSpec((tm, d // 2), lambda i: (i, 0)),
                   pl.BlockSpec((tm, LANES), lambda i: (i, 0)),
                   pl.BlockSpec((tm, LANES), lambda i: (i, 0))],
        out_shape=[jax.ShapeDtypeStruct((t, d), F32),
                   jax.ShapeDtypeStruct((t, d // 2), jnp.uint32),
                   jax.ShapeDtypeStruct((t, LANES), jnp.int32),
                   jax.ShapeDtypeStruct((t, LANES), F32)],
        compiler_params=_cparams(("arbitrary",)),
        name="out_proj",
    )(ycat, x, w, g1, ng, sc, sh, wr, br)


def _first_half(x, c, w_ada, b_ada, norm1_g, w_in_proj, conv_w, conv_b, dt_bias, a_log, d_skip, ssd_norm_g,
                w_pool, b_pool, pool_scale, w_out_proj, norm2_g, w_router, b_router):
    _, t, d = x.shape
    n_heads = dt_bias.shape[1]
    d_ssd = n_heads * SSD_HEAD_DIM
    d_conv = conv_w.shape[2]
    d_pool = b_pool.shape[1]
    n_experts = w_router.shape[2]
    x2 = x.reshape(t, d)

    mod = _ada(c, w_ada, b_ada)
    sh1, sc1, g1, sh2, sc2, g2 = [mod[:, k * d:(k + 1) * d] for k in range(6)]

    w_main, w_dt = _prep_w_in(w_in_proj, d_ssd, d_conv, n_heads, d_pool)
    tm1 = min(1024, t)
    proj, dt_raw = _in_proj(x2, norm1_g, sc1, sh1, w_main, w_dt, tm1, 1024, n_heads)

    pad_h = LANES - n_heads
    dtb = jnp.pad(dt_bias, ((0, 0), (0, pad_h)))
    alog = jnp.pad(a_log, ((0, 0), (0, pad_h)))
    dskip_x = jnp.repeat(d_skip, SSD_HEAD_DIM, axis=1)
    eh = (jnp.arange(LANES)[:, None] == (jnp.arange(d_ssd) // SSD_HEAD_DIM)[None, :]).astype(BF16)
    ycat = _ssd_pool(proj, dt_raw, conv_w[0], conv_b, dtb, alog, dskip_x, ssd_norm_g, eh,
                     w_pool[0].astype(BF16), b_pool, pool_scale, d_ssd, d_conv, d_pool, n_heads)

    wr_hi = w_router[0].astype(BF16)
    wr_mid = (w_router[0] - wr_hi.astype(F32)).astype(BF16)
    assert 2 * n_experts <= LANES
    wr = jnp.pad(jnp.concatenate([wr_hi, wr_mid], axis=1), ((0, 0), (0, LANES - 2 * n_experts)))
    br = jnp.pad(b_router, ((0, 0), (0, LANES - n_experts)))
    x1, h2p, idx, gates = _out_proj(ycat, x2, w_out_proj[0].astype(BF16), g1, norm2_g, sc2, sh2, wr, br,
                                    n_experts, min(512, t))
    return x1, h2p, idx, gates, g2


MOE_SUB = 256
MOE_ROWS_MAX = 2304
MOE_TF = 256
MOE_TAB_ALIGN = 1024
MOE_TAB = 4096


def _route_tables(idx, n_experts):
    t, k = idx.shape
    rmax, sub = MOE_ROWS_MAX, MOE_SUB
    ns_max = n_experts + (t * k + rmax - 1) // rmax
    tok = jnp.arange(t, dtype=jnp.int32)[:, None]
    key = (idx * t + tok) * k + jnp.arange(k, dtype=jnp.int32)[None, :]
    skey = jnp.sort(key.reshape(-1))
    row_tok = (skey // k) % t
    row_dst = (skey % k) * t + row_tok
    bounds = jnp.searchsorted(skey, jnp.arange(n_experts + 1, dtype=jnp.int32) * (t * k)).astype(jnp.int32)
    off, counts = bounds[:-1], bounds[1:] - bounds[:-1]
    n_sup = (counts + rmax - 1) // rmax
    per = (counts + jnp.maximum(n_sup, 1) - 1) // jnp.maximum(n_sup, 1)
    rps = jnp.maximum((per + sub - 1) // sub * sub, sub)
    sup_end = jnp.cumsum(n_sup)
    sup_base = sup_end - n_sup
    s_ids = jnp.arange(ns_max + 1, dtype=jnp.int32)
    live = s_ids < sup_end[-1]
    st_e = jnp.minimum(jnp.searchsorted(sup_end, s_ids, side="right"), n_experts - 1).astype(jnp.int32)
    s_loc = s_ids - sup_base[st_e]
    st_n = jnp.where(live, jnp.clip(counts[st_e] - s_loc * rps[st_e], 0, rps[st_e]), 0).astype(jnp.int32)
    st_start = jnp.where(live, off[st_e] + s_loc * rps[st_e], 0).astype(jnp.int32)
    n_super = sup_end[-1:].astype(jnp.int32)
    row_tok = jnp.pad(row_tok, (0, MOE_TAB))
    row_dst = jnp.pad(row_dst, (0, MOE_TAB))
    return row_tok, row_dst, st_e, st_n, st_start, n_super


def _moe_kernel(st_e, st_n, st_start, nsup, tok_hbm, dst_hbm, h2p_hbm, win_hbm, wout_hbm, bin_ref, bout_ref,
                perm_ref, y_hbm, tok_s, dst_s, xbuf, acc, ybuf, win_st, wout_st, wi_bf0, wo_bf0, wi_bf1, wo_bf1,
                act_scr,
                tsem, gsem, ssem, wsem, *, nf):
    sub, rmax, tf = MOE_SUB, MOE_ROWS_MAX, MOE_TF
    nsub_max = rmax // sub
    gsub = sub // SUBLANES
    dh = xbuf.shape[2]
    d_out = acc.shape[2]
    n_super = nsup[0]

    def n_sub_of(s):
        return (st_n[s] + sub - 1) // sub

    def tab_base(s):
        return pl.multiple_of((st_start[s] // MOE_TAB_ALIGN) * MOE_TAB_ALIGN, MOE_TAB_ALIGN)

    def tab_off(s, slot):
        return slot * MOE_TAB + st_start[s] - tab_base(s)

    def table_copies(s, slot):
        win = pl.ds(tab_base(s), MOE_TAB)
        dst = pl.ds(pl.multiple_of(slot * MOE_TAB, MOE_TAB), MOE_TAB)
        return (pltpu.make_async_copy(tok_hbm.at[win], tok_s.at[dst], tsem.at[slot]),
                pltpu.make_async_copy(dst_hbm.at[win], dst_s.at[dst], tsem.at[slot]))

    def groups(m):
        return pl.ds(pl.multiple_of(m * gsub, gsub), gsub)

    def weight_copies(e, j):
        cols = pl.ds(pl.multiple_of(j * 2 * tf, 2 * tf), 2 * tf)
        hk = win_st.shape[0] // 2
        c_in0 = pltpu.make_async_copy(win_hbm.at[e, 0:hk, cols], win_st.at[0:hk], wsem.at[0])
        c_in1 = pltpu.make_async_copy(win_hbm.at[e, hk:2 * hk, cols], win_st.at[hk:2 * hk], wsem.at[0])
        c_out = pltpu.make_async_copy(wout_hbm.at[e, pl.ds(pl.multiple_of(j * tf, tf), tf), :],
                                      wout_st, wsem.at[0])
        return (c_in0, 0), (c_in1, 1), (c_out, 0)

    def gather_rows(m, toff):
        def body(i, carry):
            g = m * gsub + i
            for u in range(SUBLANES):
                tk = tok_s[toff + g * SUBLANES + u]
                pltpu.make_async_copy(h2p_hbm.at[pl.ds(tk, 1)], xbuf.at[g, pl.ds(u, 1)],
                                      gsem.at[m]).start(priority=u % 2)
            return carry
        lax.fori_loop(0, gsub, body, 0)

    def wait_gather(m):
        pltpu.make_async_copy(xbuf.at[groups(m)], xbuf.at[groups(m)], gsem.at[m]).wait()

    def scatter_rows(m, toff, nvalid):
        def body(i, carry):
            g = m * gsub + i
            for u in range(SUBLANES):
                d = dst_s[toff + g * SUBLANES + u]
                pltpu.make_async_copy(ybuf.at[g, pl.ds(u, 1)], y_hbm.at[pl.ds(d, 1)],
                                      ssem.at[m]).start(priority=u % 2)
            return carry
        n_full = nvalid // SUBLANES
        lax.fori_loop(0, n_full, body, 0)
        g_last = m * gsub + n_full

        def tail(u, carry):
            d = dst_s[toff + g_last * SUBLANES + u]
            pltpu.make_async_copy(ybuf.at[g_last, pl.ds(u, 1)], y_hbm.at[pl.ds(d, 1)], ssem.at[m]).start()
            return carry
        lax.fori_loop(0, nvalid - n_full * SUBLANES, tail, 0)

    def wait_scatter(m, nvalid):
        @pl.when(nvalid == sub)
        def _():
            pltpu.make_async_copy(ybuf.at[groups(m)], ybuf.at[groups(m)], ssem.at[m]).wait()

        @pl.when(nvalid < sub)
        def _():
            def body(i, carry):
                pltpu.make_async_copy(ybuf.at[0, pl.ds(0, 1)], y_hbm.at[pl.ds(0, 1)], ssem.at[m]).wait()
                return carry
            lax.fori_loop(0, nvalid, body, 0)

    def valid_rows(n_rows, m):
        return jnp.clip(n_rows - m * sub, 0, sub)

    xbuf[...] = jnp.zeros_like(xbuf)
    acc[...] = jnp.zeros_like(acc)
    for cp in table_copies(0, 0):
        cp.start()
    for cp in table_copies(0, 0):
        cp.wait()

    act_a, act_b = act_scr.at[0], act_scr.at[1]

    def succ(s, j):
        last = j == nf - 1
        s1 = jnp.where(last, jnp.minimum(s + 1, n_super - 1), s)
        j1 = jnp.where(last, jnp.where(s + 1 < n_super, 0, j), j + 1)
        return s1, j1

    wbufs = ((wi_bf0, wo_bf0), (wi_bf1, wo_bf1))

    def convert(ws):
        wi, wo = wbufs[ws]
        perm = perm_ref[...]
        rblk = 512
        for cgrp in range(2 * tf // 256):
            for rb in range(0, wi.shape[0], rblk):
                wblk = win_st[rb:rb + rblk, cgrp * 256:(cgrp + 1) * 256].astype(BF16)
                pw = jnp.dot(wblk, perm, preferred_element_type=F32).astype(BF16)
                wi[rb:rb + rblk, cgrp * LANES:(cgrp + 1) * LANES] = pw[:, :LANES]
                wi[rb:rb + rblk, tf + cgrp * LANES: tf + (cgrp + 1) * LANES] = pw[:, LANES:]
        wo[...] = wout_st[...].astype(BF16)

    def stage1(m, act_ref, ws, b_in):
        wi = wbufs[ws][0]
        x_lo, x_hi = _unpack_bf16_pairs(xbuf[groups(m)].reshape(sub, dh))
        hb = (jnp.dot(x_lo, wi[0:dh, :], preferred_element_type=F32)
              + jnp.dot(x_hi, wi[dh:2 * dh, :], preferred_element_type=F32) + b_in)
        glu = jnp.minimum(hb[:, :tf], SWIGLU_LIMIT)
        lin = jnp.clip(hb[:, tf:], -SWIGLU_LIMIT, SWIGLU_LIMIT)
        act = glu * jax.nn.sigmoid(SWIGLU_ALPHA * glu) * (lin + 1.0)
        act_ref[...] = act.astype(BF16)

    def stage2(m, act_ref, ws, j):
        o = jnp.dot(act_ref[...], wbufs[ws][1][...], preferred_element_type=F32)
        prev = acc[groups(m)].reshape(sub, d_out)
        acc[groups(m)] = (jnp.where(j > 0, prev, 0.0) + o).reshape(gsub, SUBLANES, d_out)

    for cp, prio in weight_copies(st_e[0], 0):
        cp.start(priority=prio)
    toff0 = tab_off(0, 0)
    lax.fori_loop(0, n_sub_of(0), lambda m, c: (gather_rows(m, toff0), c)[1], 0)
    for cp, _ in weight_copies(st_e[0], 0):
        cp.wait()
    convert(0)
    s_nx, j_nx = succ(0, 0)
    for cp, prio in weight_copies(st_e[s_nx], j_nx):
        cp.start(priority=prio)

    def supertile(s, carry):
        e = st_e[s]
        n_rows = st_n[s]
        n_sub = n_sub_of(s)
        tslot = s % 2
        toff = tab_off(s, tslot)
        n_rows_prev = jnp.where(s > 0, st_n[jnp.maximum(s - 1, 0)], 0)
        for cp in table_copies(s + 1, 1 - tslot):
            cp.start()
        lax.fori_loop(0, n_sub, lambda m, c2: (wait_gather(m), c2)[1], 0)
        stage1(0, act_a, 0, bin_ref[e, pl.ds(0, 1), :])

        def chunk(j, ws):
            b_in = bin_ref[e, pl.ds(j, 1), :]
            n_pairs = (n_sub - 1) // 2

            def pair(p, carry):
                m = 2 * p
                stage1(m + 1, act_b, ws, b_in)
                stage2(m, act_a, ws, j)
                stage1(m + 2, act_a, ws, b_in)
                stage2(m + 1, act_b, ws, j)
                return carry
            lax.fori_loop(0, n_pairs, pair, 0)
            m_last = 2 * n_pairs

            s1, j1 = succ(s, j)
            e1 = st_e[s1]
            for cp, _ in weight_copies(e1, j1):
                cp.wait()
            b_in1 = bin_ref[e1, pl.ds(j1, 1), :]

            @pl.when(m_last < n_sub - 1)
            def _():
                stage1(m_last + 1, act_b, ws, b_in)
                stage2(m_last, act_a, ws, j)
                stage2(m_last + 1, act_b, ws, j)
                convert(1 - ws)
                stage1(0, act_a, 1 - ws, b_in1)

            @pl.when(m_last == n_sub - 1)
            def _():
                stage2(m_last, act_a, ws, j)
                convert(1 - ws)
                stage1(0, act_a, 1 - ws, b_in1)

            s2, j2 = succ(s1, j1)
            for cp, prio in weight_copies(st_e[s2], j2):
                cp.start(priority=prio)

        def chunk_pair(jj, carry):
            chunk(2 * jj, 0)
            chunk(2 * jj + 1, 1)
            return carry
        lax.fori_loop(0, nf // 2, chunk_pair, 0)

        for cp in table_copies(s + 1, 1 - tslot):
            cp.wait()
        toff_next = tab_off(s + 1, 1 - tslot)
        lax.fori_loop(0, n_sub_of(s + 1), lambda m, c: (gather_rows(m, toff_next), c)[1], 0)
        n_sub_prev = (n_rows_prev + sub - 1) // sub
        lax.fori_loop(n_sub, jnp.maximum(n_sub, n_sub_prev),
                      lambda m, c: (wait_scatter(m, valid_rows(n_rows_prev, m)), c)[1], 0)
        b_out = bout_ref[pl.ds(e, 1), :]

        def finish(m, carry):
            wait_scatter(m, valid_rows(n_rows_prev, m))
            v = acc[groups(m)].reshape(sub, d_out) + b_out
            ybuf[groups(m)] = _pack_bf16_pairs(v[:, :dh], v[:, dh:]).reshape(gsub, SUBLANES, dh)
            scatter_rows(m, toff, valid_rows(n_rows, m))
            return carry
        lax.fori_loop(0, n_sub, finish, 0)
        return carry
    lax.fori_loop(0, n_super, supertile, 0)

    for cp, _ in weight_copies(st_e[0], 0):
        cp.wait()
    n_rows_last = jnp.where(n_super > 0, st_n[jnp.maximum(n_super - 1, 0)], 0)
    for m in range(nsub_max):
        wait_scatter(m, valid_rows(n_rows_last, m))


def _moe(h2p, row_tok, row_dst, st_e, st_n, st_start, n_super, w_in, w_out, b_in_c, b_out, perm, t):
    n_experts, d, f2 = w_in.shape
    nf = f2 // (2 * MOE_TF)
    rmax = MOE_ROWS_MAX
    nsub_max = rmax // MOE_SUB
    assert rmax % MOE_SUB == 0 and MOE_SUB % SUBLANES == 0 and nf % 2 == 0
    assert MOE_TAB >= rmax + MOE_TAB_ALIGN and MOE_TAB % MOE_TAB_ALIGN == 0
    smem = pl.BlockSpec(memory_space=pltpu.SMEM)
    hbm = pl.BlockSpec(memory_space=pl.ANY)
    vmem = pl.BlockSpec(memory_space=pltpu.VMEM)
    kern = functools.partial(_moe_kernel, nf=nf)
    return pl.pallas_call(
        kern,
        in_specs=[smem, smem, smem, smem, hbm, hbm, hbm, hbm, hbm, vmem, vmem, vmem],
        out_specs=hbm,
        out_shape=jax.ShapeDtypeStruct((TOP_K * t, d // 2), jnp.uint32),
        scratch_shapes=[pltpu.SMEM((2 * MOE_TAB,), jnp.int32),
                        pltpu.SMEM((2 * MOE_TAB,), jnp.int32),
                        pltpu.VMEM((rmax // SUBLANES, SUBLANES, d // 2), jnp.uint32),
                        pltpu.VMEM((rmax // SUBLANES, SUBLANES, d), F32),
                        pltpu.VMEM((rmax // SUBLANES, SUBLANES, d // 2), jnp.uint32),
                        pltpu.VMEM((d, 2 * MOE_TF), F32),
                        pltpu.VMEM((MOE_TF, d), F32),
                        pltpu.VMEM((d, 2 * MOE_TF), BF16),
                        pltpu.VMEM((MOE_TF, d), BF16),
                        pltpu.VMEM((d, 2 * MOE_TF), BF16),
                        pltpu.VMEM((MOE_TF, d), BF16),
                        pltpu.VMEM((2, MOE_SUB, MOE_TF), BF16),
                        pltpu.SemaphoreType.DMA((2,)),
                        pltpu.SemaphoreType.DMA((nsub_max,)),
                        pltpu.SemaphoreType.DMA((nsub_max,)),
                        pltpu.SemaphoreType.DMA((1,))],
        compiler_params=pltpu.CompilerParams(vmem_limit_bytes=MOE_VMEM_LIMIT, has_side_effects=True),
        name="moe",
    )(st_e, st_n, st_start, n_super, row_tok, row_dst, h2p, w_in, w_out, b_in_c, b_out, perm)


def _combine_kernel(y0_ref, y1_ref, y2_ref, y3_ref, gate_ref, x1_ref, g2_ref, fg_ref, o_ref):
    g = gate_ref[...]
    half = y0_ref.shape[1]
    y_lo = jnp.zeros(y0_ref.shape, F32)
    y_hi = jnp.zeros(y0_ref.shape, F32)
    for k, y_ref in enumerate((y0_ref, y1_ref, y2_ref, y3_ref)):
        w = y_ref[...]
        y_lo = y_lo + pltpu.bitcast(w << 16, F32) * g[:, k:k + 1]
        y_hi = y_hi + pltpu.bitcast(w & jnp.uint32(0xFFFF0000), F32) * g[:, k:k + 1]
    x_lo = x1_ref[:, :half] + g2_ref[:, :half] * y_lo
    x_hi = x1_ref[:, half:] + g2_ref[:, half:] * y_hi
    ssq = jnp.sum(x_lo * x_lo, axis=-1, keepdims=True) + jnp.sum(x_hi * x_hi, axis=-1, keepdims=True)
    inv = lax.rsqrt(ssq / (2 * half) + NORM_EPS)
    o_ref[:, :half] = x_lo * inv * fg_ref[:, :half]
    o_ref[:, half:] = x_hi * inv * fg_ref[:, half:]


def _combine(y, gates, x1, g2, fg, tm):
    t, d = x1.shape
    nb = t // tm
    vec = pl.BlockSpec((1, d), lambda i: (0, 0))
    yspecs = [pl.BlockSpec((tm, d // 2), functools.partial(lambda i, k: (k * nb + i, 0), k=k))
              for k in range(TOP_K)]
    return pl.pallas_call(
        _combine_kernel,
        grid=(nb,),
        in_specs=yspecs + [pl.BlockSpec((tm, LANES), lambda i: (i, 0)),
                           pl.BlockSpec((tm, d), lambda i: (i, 0)), vec, vec],
        out_specs=pl.BlockSpec((tm, d), lambda i: (i, 0)),
        out_shape=jax.ShapeDtypeStruct((t, d), F32),
        compiler_params=_cparams(("arbitrary",)),
        name="combine",
    )(y, y, y, y, gates, x1, g2, fg)


def kernel(x, c, w_ada, b_ada, norm1_g, w_in_proj, conv_w, conv_b, dt_bias, a_log, d_skip, ssd_norm_g, w_pool,
           b_pool, pool_scale, w_out_proj, norm2_g, w_router, b_router, w_exp_in, b_exp_in, w_exp_out, b_exp_out,
           final_norm_g):
    assert x.shape[0] == 1 and w_ada.shape[0] == 1
    _, t, d = x.shape
    n_experts = w_router.shape[2]
    x1, h2p, idx, gates, g2 = _first_half(x, c, w_ada, b_ada, norm1_g, w_in_proj, conv_w, conv_b, dt_bias, a_log,
                                          d_skip, ssd_norm_g, w_pool, b_pool, pool_scale, w_out_proj, norm2_g,
                                          w_router, b_router)
    row_tok, row_dst, st_e, st_n, st_start, n_super = _route_tables(idx[:, :TOP_K], n_experts)
    f = w_exp_out.shape[2]
    nf = f // MOE_TF
    b_in = b_exp_in[0].reshape(n_experts, nf, MOE_TF, 2)
    b_in_c = jnp.concatenate([b_in[..., 0], b_in[..., 1]], axis=-1)
    src = jnp.concatenate([2 * jnp.arange(LANES), 2 * jnp.arange(LANES) + 1])
    perm = (jnp.arange(2 * LANES)[:, None] == src[None, :]).astype(BF16)
    y = _moe(h2p, row_tok, row_dst, st_e, st_n, st_start, n_super, w_exp_in.reshape(w_exp_in.shape[1:]),
             w_exp_out.reshape(w_exp_out.shape[1:]), b_in_c, b_exp_out.reshape(n_experts, d), perm, t)
    out = _combine(y, gates, x1, g2, final_norm_g.reshape(1, d), min(256, t))
    return out.reshape(x.shape)
```

```python
import functools

import jax
import jax.numpy as jnp
from jax import lax
from jax.experimental import pallas as pl
from jax.experimental.pallas import tpu as pltpu

F32 = jnp.float32
BF16 = jnp.bfloat16
HIGHEST = lax.Precision.HIGHEST

SSD_HEAD_DIM = 64
SSD_GROUPS = 4
D_STATE = 128
CONV_K = 4
CHUNK = 128
POOL_WINDOWS = (2, 4, 8, 16)
TOP_K = 4
SWIGLU_LIMIT = 7.0
SWIGLU_ALPHA = 1.702
NORM_EPS = 1e-6

LANES = 128
SUBLANES = 8
VMEM_LIMIT = 56 * 1024 * 1024
MOE_VMEM_LIMIT = 60 * 1024 * 1024


def _cparams(sem, vmem=VMEM_LIMIT):
    return pltpu.CompilerParams(dimension_semantics=sem, vmem_limit_bytes=vmem)


def _silu(v):
    h = 0.5 * v
    return h + h * jnp.tanh(h)


def _ada_kernel(c_ref, w_ref, b_ref, o_ref, cond_scr):
    cond_scr[...] = _silu(c_ref[...])
    d, tn = w_ref.shape

    def body(r, acc):
        rows = pl.ds(pl.multiple_of(r * SUBLANES, SUBLANES), SUBLANES)
        return acc + w_ref[rows, :] * cond_scr[rows, :]
    acc = lax.fori_loop(0, d // SUBLANES, body, jnp.zeros((SUBLANES, tn), F32), unroll=8)
    o_ref[...] = jnp.sum(acc, axis=0, keepdims=True) + b_ref[...]


def _ada(c, w, b):
    _, d, n = w.shape
    tn = 1536
    return pl.pallas_call(
        _ada_kernel,
        grid=(n // tn,),
        in_specs=[pl.BlockSpec((d, 1), lambda j: (0, 0)),
                  pl.BlockSpec((None, d, tn), lambda j: (0, 0, j)),
                  pl.BlockSpec((1, tn), lambda j: (0, j))],
        out_specs=pl.BlockSpec((1, tn), lambda j: (0, j)),
        out_shape=jax.ShapeDtypeStruct((1, n), F32),
        scratch_shapes=[pltpu.VMEM((d, 1), F32)],
        compiler_params=_cparams(("arbitrary",)),
        name="ada",
    )(c.reshape(d, 1), w, b)


def _prep_w_kernel(a_ref, dt_rows_ref, wm_ref, wdt_ref, *, n_heads):
    wm_ref[...] = a_ref[...].T.astype(BF16)

    @pl.when(pl.program_id(0) == pl.num_programs(0) - 1)
    def _():
        t0 = dt_rows_ref[...].T
        lane = lax.broadcasted_iota(jnp.int32, t0.shape, 1)
        hi = t0.astype(BF16).astype(F32)
        mid = pltpu.roll(t0 - hi, n_heads, axis=1)
        wdt_ref[...] = jnp.where(lane < n_heads, hi, jnp.where(lane < 2 * n_heads, mid, 0.0)).astype(BF16)


def _prep_w_in(w, d_ssd, d_conv, n_heads, d_pool):
    _, d, n_in = w.shape
    tn = d_pool
    assert d_ssd % tn == 0 and d_conv % tn == 0 and n_in == d_ssd + d_conv + n_heads + d_pool
    assert 2 * n_heads <= LANES and (d_ssd + d_conv) % LANES == 0 and n_heads % SUBLANES == 0
    nz, nx = d_ssd // tn, d_conv // tn
    n_main = nz + nx + 1
    wt = jnp.swapaxes(w, 1, 2).reshape(n_in, d)

    def row_off(j):
        t8 = tn // SUBLANES
        r8 = jnp.where(j < nx, d_ssd // SUBLANES + j * t8,
                       jnp.where(j < nx + nz, (j - nx) * t8, (d_ssd + d_conv + n_heads) // SUBLANES))
        return r8 * SUBLANES
    return pl.pallas_call(
        functools.partial(_prep_w_kernel, n_heads=n_heads),
        grid=(n_main,),
        in_specs=[pl.BlockSpec((pl.Element(tn), pl.Element(d)), lambda j: (row_off(j), 0)),
                  pl.BlockSpec((LANES, d), lambda j: ((d_ssd + d_conv) // LANES, 0))],
        out_specs=[pl.BlockSpec((d, tn), lambda j: (0, j)),
                   pl.BlockSpec((d, LANES), lambda j: (0, 0))],
        out_shape=[jax.ShapeDtypeStruct((d, n_main * tn), BF16), jax.ShapeDtypeStruct((d, LANES), BF16)],
        compiler_params=_cparams(("arbitrary",)),
        name="prep_w_in",
    )(wt, wt)


def _inproj_kernel(x_ref, g_ref, sc_ref, sh_ref, w_ref, wdt_ref, proj_ref, dt_ref, h_scr, *, n_heads):
    @pl.when(pl.program_id(1) == 0)
    def _():
        x = x_ref[...]
        ms = jnp.mean(x * x, axis=-1, keepdims=True)
        h = x * lax.rsqrt(ms + NORM_EPS) * g_ref[...]
        h = h * (1.0 + sc_ref[...]) + sh_ref[...]
        h_hi = h.astype(BF16)
        h_scr[...] = h_hi
        h_mid = (h - h_hi.astype(F32)).astype(BF16)
        wdt = wdt_ref[...]
        r_hi = jnp.dot(h_hi, wdt, preferred_element_type=F32)
        r_mid = jnp.dot(h_mid, wdt, preferred_element_type=F32)
        dt_ref[...] = (r_hi + r_mid) + pltpu.roll(r_hi, LANES - n_heads, axis=1)

    proj_ref[...] = jnp.dot(h_scr[...], w_ref[...], preferred_element_type=F32)


def _in_proj(x, g, sc, sh, w, wdt, tm, tn, n_heads):
    t, d = x.shape
    n = w.shape[1]
    vec = pl.BlockSpec((1, d), lambda i, j: (0, 0))
    return pl.pallas_call(
        functools.partial(_inproj_kernel, n_heads=n_heads),
        grid=(t // tm, n // tn),
        in_specs=[pl.BlockSpec((tm, d), lambda i, j: (i, 0)), vec, vec, vec,
                  pl.BlockSpec((d, tn), lambda i, j: (0, j)),
                  pl.BlockSpec((d, LANES), lambda i, j: (0, 0))],
        out_specs=[pl.BlockSpec((tm, tn), lambda i, j: (i, j)),
                   pl.BlockSpec((tm, LANES), lambda i, j: (i, 0))],
        out_shape=[jax.ShapeDtypeStruct((t, n), F32), jax.ShapeDtypeStruct((t, LANES), F32)],
        scratch_shapes=[pltpu.VMEM((tm, d), BF16)],
        compiler_params=_cparams(("arbitrary", "arbitrary")),
        name="in_proj",
    )(x, g, sc, sh, w, wdt)


def _ssd_pool_kernel(xbc_ref, xbch_ref, z0_ref, z1_ref, z2_ref, u_ref, uh_ref, dt_ref,
                     cw_ref, cb_ref, dtb_ref, alog_ref, dskip_ref, ng_ref, eh_ref,
                     wp_ref, bp_ref, ps_ref, y_ref,
                     st_scr, xe_scr, xa_scr, ue_scr, yd_scr, *, d_ssd, n_heads):
    i = pl.program_id(0)
    L = CHUNK
    gw = d_ssd // SSD_GROUPS
    halo_c = xe_scr.shape[0] - L
    halo_u = ue_scr.shape[0] - L
    first = i == 0

    @pl.when(first)
    def _():
        st_scr[...] = jnp.zeros_like(st_scr)

    xe_scr[0:halo_c, :] = jnp.where(first, 0.0, xbch_ref[...])
    xe_scr[halo_c:, :] = xbc_ref[...]
    ncols = xe_scr.shape[1]
    cblk = 512
    for c0 in range(0, ncols, cblk):
        xe = xe_scr[:, c0:c0 + cblk]
        xe2 = pltpu.roll(xe, 2, axis=0)
        w = [cw_ref[k:k + 1, c0:c0 + cblk] for k in range(CONV_K)]
        even = cb_ref[:, c0:c0 + cblk] + w[3] * xe[halo_c:, :] + w[1] * xe2[halo_c:, :]
        odd = pltpu.roll(w[2] * xe + w[0] * xe2, 1, axis=0)[halo_c:, :]
        xa_scr[:, c0:c0 + cblk] = _silu(even + odd)

    lane = lax.broadcasted_iota(jnp.int32, (L, LANES), 1)
    row = lax.broadcasted_iota(jnp.int32, (L, LANES), 0)
    head_ok = lane < n_heads
    dt = jnp.where(head_ok, jax.nn.softplus(dt_ref[...] + dtb_ref[...]), 0.0)
    a = -jnp.exp(alog_ref[...])
    da = dt * a
    causal = row >= lane
    tri = jnp.where(causal, 1.0, 0.0).astype(F32)
    cum = jnp.dot(tri, da, preferred_element_type=F32, precision=HIGHEST)
    cum_t = cum.T
    dt_t = dt.T
    cum_last = cum[L - 1:L, :]
    ecum = jnp.exp(cum)
    wend = jnp.exp(cum_last - cum) * dt
    eh = eh_ref[...]
    ecum_x = jnp.dot(ecum.astype(BF16), eh, preferred_element_type=F32)
    wend_x = jnp.dot(wend.astype(BF16), eh, preferred_element_type=F32)

    hpg = n_heads // SSD_GROUPS
    for g in range(SSD_GROUPS):
        b_g = xa_scr[:, d_ssd + g * D_STATE: d_ssd + (g + 1) * D_STATE]
        c_g = xa_scr[:, d_ssd + SSD_GROUPS * D_STATE + g * D_STATE:
                     d_ssd + SSD_GROUPS * D_STATE + (g + 1) * D_STATE]
        b_bf = b_g.astype(BF16)
        c_bf = c_g.astype(BF16)
        cb = lax.dot_general(c_bf, b_bf, (((1,), (1,)), ((), ())), preferred_element_type=F32)
        for hp in range(hpg // 2):
            h0 = g * hpg + 2 * hp
            ms = []
            for h in (h0, h0 + 1):
                seg = cum[:, h:h + 1] - cum_t[h:h + 1, :]
                decay = jnp.exp(jnp.where(causal, seg, -jnp.inf))
                ms.append(cb * decay * dt_t[h:h + 1, :])
            lhs = jnp.concatenate(ms, axis=1).astype(BF16)
            c0 = h0 * SSD_HEAD_DIM
            xpair = xa_scr[:, c0:c0 + LANES]
            top = jnp.where(lane < SSD_HEAD_DIM, xpair, 0.0)
            bot = jnp.where(lane >= SSD_HEAD_DIM, xpair, 0.0)
            rhs = jnp.concatenate([top, bot], axis=0).astype(BF16)
            yd_scr[:, c0:c0 + LANES] = jnp.dot(lhs, rhs, preferred_element_type=F32)
        gs = slice(g * gw, (g + 1) * gw)
        st = st_scr[g]
        y_off = jnp.dot(c_bf, st.astype(BF16), preferred_element_type=F32) * ecum_x[:, gs]
        xs_g = xa_scr[:, gs]
        y_g = yd_scr[:, gs] + y_off + dskip_ref[:, gs] * xs_g
        xw = (xs_g * wend_x[:, gs]).astype(BF16)
        st_scr[g] = st * ecum_x[L - 1:L, gs] + jnp.dot(b_g.T.astype(BF16), xw, preferred_element_type=F32)
        zparts = (z0_ref, z1_ref, z2_ref)
        zw = z0_ref.shape[1]
        zg = jnp.concatenate(
            [zparts[(g * gw + o) // zw][:, (g * gw + o) % zw:(g * gw + o) % zw + LANES] for o in range(0, gw, LANES)],
            axis=1)
        y_g = y_g * _silu(zg)
        msq = jnp.mean(y_g * y_g, axis=-1, keepdims=True)
        y_ref[:, gs] = (y_g * lax.rsqrt(msq + NORM_EPS) * ng_ref[:, gs]).astype(y_ref.dtype)

    ue_scr[0:halo_u, :] = jnp.where(first, 0.0, uh_ref[...])
    ue_scr[halo_u:, :] = u_ref[...]
    pg = ue_scr.shape[1] // len(POOL_WINDOWS)
    tpos = (i * L + lax.broadcasted_iota(jnp.int32, (L, pg), 0) + 1).astype(F32)
    for gi, w in enumerate(POOL_WINDOWS):
        cs = slice(gi * pg, (gi + 1) * pg)
        ue = ue_scr[:, cs]
        win = ue
        n = 1
        while n < w:
            win = win + pltpu.roll(win, n, axis=0)
            n *= 2
        win = win[halo_u:, :]
        tok = ue[halo_u:, :]
        pooled = win / jnp.minimum(tpos, float(w)) - tok
        yp = jnp.dot(pooled.astype(BF16), wp_ref[gi], preferred_element_type=F32)
        y_ref[:, d_ssd + gi * pg: d_ssd + (gi + 1) * pg] = ((yp + bp_ref[:, cs]) * ps_ref[:, cs]).astype(y_ref.dtype)


def _ssd_pool(proj, dt, cw, cb, dtb, alog, dskip_x, ng, eh, wp, bp, ps, d_ssd, d_conv, d_pool, n_heads):
    t = proj.shape[0]
    L = CHUNK
    halo_c, halo_u = 8, 16
    assert CONV_K == 4 and CONV_K - 1 <= halo_c
    assert all(w & (w - 1) == 0 and w <= halo_u for w in POOL_WINDOWS)
    zw = 1024
    zb = d_conv // zw
    ub = (d_conv + d_ssd) // d_pool
    d_mix = d_ssd + d_pool
    gw = d_ssd // SSD_GROUPS

    def full(shape):
        return pl.BlockSpec(shape, lambda i: (0,) * len(shape))

    in_specs = [
        pl.BlockSpec((L, d_conv), lambda i: (i, 0)),
        pl.BlockSpec((halo_c, d_conv), lambda i: (jnp.maximum(i * (L // halo_c) - 1, 0), 0)),
        pl.BlockSpec((L, zw), lambda i: (i, zb)),
        pl.BlockSpec((L, zw), lambda i: (i, zb + 1)),
        pl.BlockSpec((L, zw), lambda i: (i, zb + 2)),
        pl.BlockSpec((L, d_pool), lambda i: (i, ub)),
        pl.BlockSpec((halo_u, d_pool), lambda i: (jnp.maximum(i * (L // halo_u) - 1, 0), ub)),
        pl.BlockSpec((L, LANES), lambda i: (i, 0)),
        full(cw.shape), full(cb.shape), full(dtb.shape), full(alog.shape), full(dskip_x.shape),
        full(ng.shape), full(eh.shape), full(wp.shape), full(bp.shape), full(ps.shape),
    ]
    kern = functools.partial(_ssd_pool_kernel, d_ssd=d_ssd, n_heads=n_heads)
    return pl.pallas_call(
        kern,
        grid=(t // L,),
        in_specs=in_specs,
        out_specs=pl.BlockSpec((L, d_mix), lambda i: (i, 0)),
        out_shape=jax.ShapeDtypeStruct((t, d_mix), BF16),
        scratch_shapes=[pltpu.VMEM((SSD_GROUPS, D_STATE, gw), F32),
                        pltpu.VMEM((L + halo_c, d_conv), F32),
                        pltpu.VMEM((L, d_conv), F32),
                        pltpu.VMEM((L + halo_u, d_pool), F32),
                        pltpu.VMEM((L, d_ssd), F32)],
        compiler_params=_cparams(("arbitrary",)),
        name="ssd_pool",
    )(proj, proj, proj, proj, proj, proj, proj, dt, cw, cb, dtb, alog, dskip_x, ng, eh, wp, bp, ps)


def _pack_bf16_pairs(lo, hi):
    lo_b = pltpu.bitcast(lo.astype(BF16).astype(F32), jnp.uint32) >> 16
    hi_b = pltpu.bitcast(hi.astype(BF16).astype(F32), jnp.uint32) & jnp.uint32(0xFFFF0000)
    return lo_b | hi_b


def _unpack_bf16_pairs(w):
    lo = pltpu.bitcast(w << 16, F32).astype(BF16)
    hi = pltpu.bitcast(w & jnp.uint32(0xFFFF0000), F32).astype(BF16)
    return lo, hi


def _outproj_kernel(y_ref, x_ref, w_ref, g1_ref, ng_ref, sc_ref, sh_ref, wr_ref, br_ref,
                    x1_ref, h2p_ref, idx_ref, gate_ref, *, n_experts):
    mix = jnp.dot(y_ref[...], w_ref[...], preferred_element_type=F32)
    x1 = x_ref[...] + g1_ref[...] * mix
    x1_ref[...] = x1
    ms = jnp.mean(x1 * x1, axis=-1, keepdims=True)
    h = x1 * lax.rsqrt(ms + NORM_EPS) * ng_ref[...]
    h = h * (1.0 + sc_ref[...]) + sh_ref[...]
    half = h.shape[1] // 2
    h2p_ref[...] = _pack_bf16_pairs(h[:, :half], h[:, half:])

    h_hi = h.astype(BF16)
    h_mid = (h - h_hi.astype(F32)).astype(BF16)
    wr = wr_ref[...]
    r_hi = jnp.dot(h_hi, wr, preferred_element_type=F32)
    r_mid = jnp.dot(h_mid, wr, preferred_element_type=F32)
    logits = (r_hi + r_mid) + pltpu.roll(r_hi, LANES - n_experts, axis=1) + br_ref[...]
    tm = logits.shape[0]
    lane = lax.broadcasted_iota(jnp.int32, (tm, LANES), 1)
    lane_f = lane.astype(F32)
    vals = jnp.where(lane < n_experts, logits, -jnp.inf)
    top_v, top_i = [], []
    for _ in range(TOP_K):
        m = jnp.max(vals, axis=-1, keepdims=True)
        am = jnp.min(jnp.where(vals == m, lane_f, float(LANES)), axis=-1, keepdims=True)
        top_v.append(m)
        top_i.append(am)
        vals = jnp.where(lane_f == am, -jnp.inf, vals)
    es = [jnp.exp(v - top_v[0]) for v in top_v]
    denom = es[0] + es[1] + es[2] + es[3]
    idx_out = jnp.zeros((tm, LANES), F32)
    gate_out = jnp.zeros((tm, LANES), F32)
    for k in range(TOP_K):
        idx_out = jnp.where(lane == k, top_i[k], idx_out)
        gate_out = jnp.where(lane == k, es[k] / denom, gate_out)
    idx_ref[...] = idx_out.astype(jnp.int32)
    gate_ref[...] = gate_out


def _out_proj(ycat, x, w, g1, ng, sc, sh, wr, br, n_experts, tm):
    t, d = x.shape
    dm = ycat.shape[1]
    vec = pl.BlockSpec((1, d), lambda i: (0, 0))
    kern = functools.partial(_outproj_kernel, n_experts=n_experts)
    return pl.pallas_call(
        kern,
        grid=(t // tm,),
        in_specs=[pl.BlockSpec((tm, dm), lambda i: (i, 0)),
                  pl.BlockSpec((tm, d), lambda i: (i, 0)),
                  pl.BlockSpec((dm, d), lambda i: (0, 0), pipeline_mode=pl.Buffered(1)),
                  vec, vec, vec, vec,
                  pl.BlockSpec((d, LANES), lambda i: (0, 0)),
                  pl.BlockSpec((1, LANES), lambda i: (0, 0))],
        out_specs=[pl.BlockSpec((tm, d), lambda i: (i, 0)),
                   pl.BlockSpec((tm, d // 2), lambda i: (i, 0)),
                   pl.BlockSpec((tm, LANES), lambda i: (i, 0)),
                   pl.BlockSpec((tm, LANES), lambda i: (i, 0))],
        out_shape=[jax.ShapeDtypeStruct((t, d), F32),
                   jax.ShapeDtypeStruct((t, d // 2), jnp.uint32),
                   jax.ShapeDtypeStruct((t, LANES), jnp.int32),
                   jax.ShapeDtypeStruct((t, LANES), F32)],
        compiler_params=_cparams(("arbitrary",)),
        name="out_proj",
    )(ycat, x, w, g1, ng, sc, sh, wr, br)


def _first_half(x, c, w_ada, b_ada, norm1_g, w_in_proj, conv_w, conv_b, dt_bias, a_log, d_skip, ssd_norm_g,
                w_pool, b_pool, pool_scale, w_out_proj, norm2_g, w_router, b_router):
    _, t, d = x.shape
    n_heads = dt_bias.shape[1]
    d_ssd = n_heads * SSD_HEAD_DIM
    d_conv = conv_w.shape[2]
    d_pool = b_pool.shape[1]
    n_experts = w_router.shape[2]
    x2 = x.reshape(t, d)

    mod = _ada(c, w_ada, b_ada)
    sh1, sc1, g1, sh2, sc2, g2 = [mod[:, k * d:(k + 1) * d] for k in range(6)]

    w_main, w_dt = _prep_w_in(w_in_proj, d_ssd, d_conv, n_heads, d_pool)
    tm1 = min(1024, t)
    proj, dt_raw = _in_proj(x2, norm1_g, sc1, sh1, w_main, w_dt, tm1, 1024, n_heads)

    pad_h = LANES - n_heads
    dtb = jnp.pad(dt_bias, ((0, 0), (0, pad_h)))
    alog = jnp.pad(a_log, ((0, 0), (0, pad_h)))
    dskip_x = jnp.repeat(d_skip, SSD_HEAD_DIM, axis=1)
    eh = (jnp.arange(LANES)[:, None] == (jnp.arange(d_ssd) // SSD_HEAD_DIM)[None, :]).astype(BF16)
    ycat = _ssd_pool(proj, dt_raw, conv_w[0], conv_b, dtb, alog, dskip_x, ssd_norm_g, eh,
                     w_pool[0].astype(BF16), b_pool, pool_scale, d_ssd, d_conv, d_pool, n_heads)

    wr_hi = w_router[0].astype(BF16)
    wr_mid = (w_router[0] - wr_hi.astype(F32)).astype(BF16)
    assert 2 * n_experts <= LANES
    wr = jnp.pad(jnp.concatenate([wr_hi, wr_mid], axis=1), ((0, 0), (0, LANES - 2 * n_experts)))
    br = jnp.pad(b_router, ((0, 0), (0, LANES - n_experts)))
    x1, h2p, idx, gates = _out_proj(ycat, x2, w_out_proj[0].astype(BF16), g1, norm2_g, sc2, sh2, wr, br,
                                    n_experts, min(512, t))
    return x1, h2p, idx, gates, g2


MOE_SUB = 256
MOE_ROWS_MAX = 2304
MOE_TF = 256
MOE_TAB_ALIGN = 1024
MOE_TAB = 4096


def _route_tables(idx, n_experts):
    t, k = idx.shape
    rmax, sub = MOE_ROWS_MAX, MOE_SUB
    ns_max = n_experts + (t * k + rmax - 1) // rmax
    tok = jnp.arange(t, dtype=jnp.int32)[:, None]
    key = (idx * t + tok) * k + jnp.arange(k, dtype=jnp.int32)[None, :]
    skey = jnp.sort(key.reshape(-1))
    row_tok = (skey // k) % t
    row_dst = (skey % k) * t + row_tok
    bounds = jnp.searchsorted(skey, jnp.arange(n_experts + 1, dtype=jnp.int32) * (t * k)).astype(jnp.int32)
    off, counts = bounds[:-1], bounds[1:] - bounds[:-1]
    n_sup = (counts + rmax - 1) // rmax
    per = (counts + jnp.maximum(n_sup, 1) - 1) // jnp.maximum(n_sup, 1)
    rps = jnp.maximum((per + sub - 1) // sub * sub, sub)
    sup_end = jnp.cumsum(n_sup)
    sup_base = sup_end - n_sup
    s_ids = jnp.arange(ns_max + 1, dtype=jnp.int32)
    live = s_ids < sup_end[-1]
    st_e = jnp.minimum(jnp.searchsorted(sup_end, s_ids, side="right"), n_experts - 1).astype(jnp.int32)
    s_loc = s_ids - sup_base[st_e]
    st_n = jnp.where(live, jnp.clip(counts[st_e] - s_loc * rps[st_e], 0, rps[st_e]), 0).astype(jnp.int32)
    st_start = jnp.where(live, off[st_e] + s_loc * rps[st_e], 0).astype(jnp.int32)
    n_super = sup_end[-1:].astype(jnp.int32)
    row_tok = jnp.pad(row_tok, (0, MOE_TAB))
    row_dst = jnp.pad(row_dst, (0, MOE_TAB))
    return row_tok, row_dst, st_e, st_n, st_start, n_super


def _moe_kernel(st_e, st_n, st_start, nsup, tok_hbm, dst_hbm, h2p_hbm, win_hbm, wout_hbm, bin_ref, bout_ref,
                perm_ref, y_hbm, tok_s, dst_s, xbuf, acc, ybuf, win_st, wout_st, wi_bf0, wo_bf0, wi_bf1, wo_bf1,
                act_scr,
                tsem, gsem, ssem, wsem, *, nf):
    sub, rmax, tf = MOE_SUB, MOE_ROWS_MAX, MOE_TF
    nsub_max = rmax // sub
    gsub = sub // SUBLANES
    dh = xbuf.shape[2]
    d_out = acc.shape[2]
    n_super = nsup[0]

    def n_sub_of(s):
        return (st_n[s] + sub - 1) // sub

    def tab_base(s):
        return pl.multiple_of((st_start[s] // MOE_TAB_ALIGN) * MOE_TAB_ALIGN, MOE_TAB_ALIGN)

    def tab_off(s, slot):
        return slot * MOE_TAB + st_start[s] - tab_base(s)

    def table_copies(s, slot):
        win = pl.ds(tab_base(s), MOE_TAB)
        dst = pl.ds(pl.multiple_of(slot * MOE_TAB, MOE_TAB), MOE_TAB)
        return (pltpu.make_async_copy(tok_hbm.at[win], tok_s.at[dst], tsem.at[slot]),
                pltpu.make_async_copy(dst_hbm.at[win], dst_s.at[dst], tsem.at[slot]))

    def groups(m):
        return pl.ds(pl.multiple_of(m * gsub, gsub), gsub)

    def weight_copies(e, j):
        cols = pl.ds(pl.multiple_of(j * 2 * tf, 2 * tf), 2 * tf)
        hk = win_st.shape[0] // 2
        c_in0 = pltpu.make_async_copy(win_hbm.at[e, 0:hk, cols], win_st.at[0:hk], wsem.at[0])
        c_in1 = pltpu.make_async_copy(win_hbm.at[e, hk:2 * hk, cols], win_st.at[hk:2 * hk], wsem.at[0])
        c_out = pltpu.make_async_copy(wout_hbm.at[e, pl.ds(pl.multiple_of(j * tf, tf), tf), :],
                                      wout_st, wsem.at[0])
        return (c_in0, 0), (c_in1, 1), (c_out, 0)

    def gather_rows(m, toff):
        def body(i, carry):
            g = m * gsub + i
            for u in range(SUBLANES):
                tk = tok_s[toff + g * SUBLANES + u]
                pltpu.make_async_copy(h2p_hbm.at[pl.ds(tk, 1)], xbuf.at[g, pl.ds(u, 1)],
                                      gsem.at[m]).start(priority=u % 2)
            return carry
        lax.fori_loop(0, gsub, body, 0)

    def wait_gather(m):
        pltpu.make_async_copy(xbuf.at[groups(m)], xbuf.at[groups(m)], gsem.at[m]).wait()

    def scatter_rows(m, toff, nvalid):
        def body(i, carry):
            g = m * gsub + i
            for u in range(SUBLANES):
                d = dst_s[toff + g * SUBLANES + u]
                pltpu.make_async_copy(ybuf.at[g, pl.ds(u, 1)], y_hbm.at[pl.ds(d, 1)],
                                      ssem.at[m]).start(priority=u % 2)
            return carry
        n_full = nvalid // SUBLANES
        lax.fori_loop(0, n_full, body, 0)
        g_last = m * gsub + n_full

        def tail(u, carry):
            d = dst_s[toff + g_last * SUBLANES + u]
            pltpu.make_async_copy(ybuf.at[g_last, pl.ds(u, 1)], y_hbm.at[pl.ds(d, 1)], ssem.at[m]).start()
            return carry
        lax.fori_loop(0, nvalid - n_full * SUBLANES, tail, 0)

    def wait_scatter(m, nvalid):
        @pl.when(nvalid == sub)
        def _():
            pltpu.make_async_copy(ybuf.at[groups(m)], ybuf.at[groups(m)], ssem.at[m]).wait()

        @pl.when(nvalid < sub)
        def _():
            def body(i, carry):
                pltpu.make_async_copy(ybuf.at[0, pl.ds(0, 1)], y_hbm.at[pl.ds(0, 1)], ssem.at[m]).wait()
                return carry
            lax.fori_loop(0, nvalid, body, 0)

    def valid_rows(n_rows, m):
        return jnp.clip(n_rows - m * sub, 0, sub)

    xbuf[...] = jnp.zeros_like(xbuf)
    acc[...] = jnp.zeros_like(acc)
    for cp in table_copies(0, 0):
        cp.start()
    for cp in table_copies(0, 0):
        cp.wait()

    act_a, act_b = act_scr.at[0], act_scr.at[1]

    def succ(s, j):
        last = j == nf - 1
        s1 = jnp.where(last, jnp.minimum(s + 1, n_super - 1), s)
        j1 = jnp.where(last, jnp.where(s + 1 < n_super, 0, j), j + 1)
        return s1, j1

    wbufs = ((wi_bf0, wo_bf0), (wi_bf1, wo_bf1))

    def convert(ws):
        wi, wo = wbufs[ws]
        perm = perm_ref[...]
        rblk = 512
        for cgrp in range(2 * tf // 256):
            for rb in range(0, wi.shape[0], rblk):
                wblk = win_st[rb:rb + rblk, cgrp * 256:(cgrp + 1) * 256].astype(BF16)
                pw = jnp.dot(wblk, perm, preferred_element_type=F32).astype(BF16)
                wi[rb:rb + rblk, cgrp * LANES:(cgrp + 1) * LANES] = pw[:, :LANES]
                wi[rb:rb + rblk, tf + cgrp * LANES: tf + (cgrp + 1) * LANES] = pw[:, LANES:]
        wo[...] = wout_st[...].astype(BF16)

    def stage1(m, act_ref, ws, b_in):
        wi = wbufs[ws][0]
        x_lo, x_hi = _unpack_bf16_pairs(xbuf[groups(m)].reshape(sub, dh))
        hb = (jnp.dot(x_lo, wi[0:dh, :], preferred_element_type=F32)
              + jnp.dot(x_hi, wi[dh:2 * dh, :], preferred_element_type=F32) + b_in)
        glu = jnp.minimum(hb[:, :tf], SWIGLU_LIMIT)
        lin = jnp.clip(hb[:, tf:], -SWIGLU_LIMIT, SWIGLU_LIMIT)
        act = glu * jax.nn.sigmoid(SWIGLU_ALPHA * glu) * (lin + 1.0)
        act_ref[...] = act.astype(BF16)

    def stage2(m, act_ref, ws, j):
        o = jnp.dot(act_ref[...], wbufs[ws][1][...], preferred_element_type=F32)
        prev = acc[groups(m)].reshape(sub, d_out)
        acc[groups(m)] = (jnp.where(j > 0, prev, 0.0) + o).reshape(gsub, SUBLANES, d_out)

    for cp, prio in weight_copies(st_e[0], 0):
        cp.start(priority=prio)
    toff0 = tab_off(0, 0)
    lax.fori_loop(0, n_sub_of(0), lambda m, c: (gather_rows(m, toff0), c)[1], 0)
    for cp, _ in weight_copies(st_e[0], 0):
        cp.wait()
    convert(0)
    s_nx, j_nx = succ(0, 0)
    for cp, prio in weight_copies(st_e[s_nx], j_nx):
        cp.start(priority=prio)

    def supertile(s, carry):
        e = st_e[s]
        n_rows = st_n[s]
        n_sub = n_sub_of(s)
        tslot = s % 2
        toff = tab_off(s, tslot)
        n_rows_prev = jnp.where(s > 0, st_n[jnp.maximum(s - 1, 0)], 0)
        for cp in table_copies(s + 1, 1 - tslot):
            cp.start()
        lax.fori_loop(0, n_sub, lambda m, c2: (wait_gather(m), c2)[1], 0)
        stage1(0, act_a, 0, bin_ref[e, pl.ds(0, 1), :])

        def chunk(j, ws):
            b_in = bin_ref[e, pl.ds(j, 1), :]
            n_pairs = (n_sub - 1) // 2

            def pair(p, carry):
                m = 2 * p
                stage1(m + 1, act_b, ws, b_in)
                stage2(m, act_a, ws, j)
                stage1(m + 2, act_a, ws, b_in)
                stage2(m + 1, act_b, ws, j)
                return carry
            lax.fori_loop(0, n_pairs, pair, 0)
            m_last = 2 * n_pairs

            s1, j1 = succ(s, j)
            e1 = st_e[s1]
            for cp, _ in weight_copies(e1, j1):
                cp.wait()
            b_in1 = bin_ref[e1, pl.ds(j1, 1), :]

            @pl.when(m_last < n_sub - 1)
            def _():
                stage1(m_last + 1, act_b, ws, b_in)
                stage2(m_last, act_a, ws, j)
                stage2(m_last + 1, act_b, ws, j)
                convert(1 - ws)
                stage1(0, act_a, 1 - ws, b_in1)

            @pl.when(m_last == n_sub - 1)
            def _():
                stage2(m_last, act_a, ws, j)
                convert(1 - ws)
                stage1(0, act_a, 1 - ws, b_in1)

            s2, j2 = succ(s1, j1)
            for cp, prio in weight_copies(st_e[s2], j2):
                cp.start(priority=prio)

        def chunk_pair(jj, carry):
            chunk(2 * jj, 0)
            chunk(2 * jj + 1, 1)
            return carry
        lax.fori_loop(0, nf // 2, chunk_pair, 0)

        for cp in table_copies(s + 1, 1 - tslot):
            cp.wait()
        toff_next = tab_off(s + 1, 1 - tslot)
        lax.fori_loop(0, n_sub_of(s + 1), lambda m, c: (gather_rows(m, toff_next), c)[1], 0)
        n_sub_prev = (n_rows_prev + sub - 1) // sub
        lax.fori_loop(n_sub, jnp.maximum(n_sub, n_sub_prev),
                      lambda m, c: (wait_scatter(m, valid_rows(n_rows_prev, m)), c)[1], 0)
        b_out = bout_ref[pl.ds(e, 1), :]

        def finish(m, carry):
            wait_scatter(m, valid_rows(n_rows_prev, m))
            v = acc[groups(m)].reshape(sub, d_out) + b_out
            ybuf[groups(m)] = _pack_bf16_pairs(v[:, :dh], v[:, dh:]).reshape(gsub, SUBLANES, dh)
            scatter_rows(m, toff, valid_rows(n_rows, m))
            return carry
        lax.fori_loop(0, n_sub, finish, 0)
        return carry
    lax.fori_loop(0, n_super, supertile, 0)

    for cp, _ in weight_copies(st_e[0], 0):
        cp.wait()
    n_rows_last = jnp.where(n_super > 0, st_n[jnp.maximum(n_super - 1, 0)], 0)
    for m in range(nsub_max):
        wait_scatter(m, valid_rows(n_rows_last, m))


def _moe(h2p, row_tok, row_dst, st_e, st_n, st_start, n_super, w_in, w_out, b_in_c, b_out, perm, t):
    n_experts, d, f2 = w_in.shape
    nf = f2 // (2 * MOE_TF)
    rmax = MOE_ROWS_MAX
    nsub_max = rmax // MOE_SUB
    assert rmax % MOE_SUB == 0 and MOE_SUB % SUBLANES == 0 and nf % 2 == 0
    assert MOE_TAB >= rmax + MOE_TAB_ALIGN and MOE_TAB % MOE_TAB_ALIGN == 0
    smem = pl.BlockSpec(memory_space=pltpu.SMEM)
    hbm = pl.BlockSpec(memory_space=pl.ANY)
    vmem = pl.BlockSpec(memory_space=pltpu.VMEM)
    kern = functools.partial(_moe_kernel, nf=nf)
    return pl.pallas_call(
        kern,
        in_specs=[smem, smem, smem, smem, hbm, hbm, hbm, hbm, hbm, vmem, vmem, vmem],
        out_specs=hbm,
        out_shape=jax.ShapeDtypeStruct((TOP_K * t, d // 2), jnp.uint32),
        scratch_shapes=[pltpu.SMEM((2 * MOE_TAB,), jnp.int32),
                        pltpu.SMEM((2 * MOE_TAB,), jnp.int32),
                        pltpu.VMEM((rmax // SUBLANES, SUBLANES, d // 2), jnp.uint32),
                        pltpu.VMEM((rmax // SUBLANES, SUBLANES, d), F32),
                        pltpu.VMEM((rmax // SUBLANES, SUBLANES, d // 2), jnp.uint32),
                        pltpu.VMEM((d, 2 * MOE_TF), F32),
                        pltpu.VMEM((MOE_TF, d), F32),
                        pltpu.VMEM((d, 2 * MOE_TF), BF16),
                        pltpu.VMEM((MOE_TF, d), BF16),
                        pltpu.VMEM((d, 2 * MOE_TF), BF16),
                        pltpu.VMEM((MOE_TF, d), BF16),
                        pltpu.VMEM((2, MOE_SUB, MOE_TF), BF16),
                        pltpu.SemaphoreType.DMA((2,)),
                        pltpu.SemaphoreType.DMA((nsub_max,)),
                        pltpu.SemaphoreType.DMA((nsub_max,)),
                        pltpu.SemaphoreType.DMA((1,))],
        compiler_params=pltpu.CompilerParams(vmem_limit_bytes=MOE_VMEM_LIMIT, has_side_effects=True),
        name="moe",
    )(st_e, st_n, st_start, n_super, row_tok, row_dst, h2p, w_in, w_out, b_in_c, b_out, perm)


def _combine_kernel(y0_ref, y1_ref, y2_ref, y3_ref, gate_ref, x1_ref, g2_ref, fg_ref, o_ref):
    g = gate_ref[...]
    half = y0_ref.shape[1]
    y_lo = jnp.zeros(y0_ref.shape, F32)
    y_hi = jnp.zeros(y0_ref.shape, F32)
    for k, y_ref in enumerate((y0_ref, y1_ref, y2_ref, y3_ref)):
        w = y_ref[...]
        y_lo = y_lo + pltpu.bitcast(w << 16, F32) * g[:, k:k + 1]
        y_hi = y_hi + pltpu.bitcast(w & jnp.uint32(0xFFFF0000), F32) * g[:, k:k + 1]
    x_lo = x1_ref[:, :half] + g2_ref[:, :half] * y_lo
    x_hi = x1_ref[:, half:] + g2_ref[:, half:] * y_hi
    ssq = jnp.sum(x_lo * x_lo, axis=-1, keepdims=True) + jnp.sum(x_hi * x_hi, axis=-1, keepdims=True)
    inv = lax.rsqrt(ssq / (2 * half) + NORM_EPS)
    o_ref[:, :half] = x_lo * inv * fg_ref[:, :half]
    o_ref[:, half:] = x_hi * inv * fg_ref[:, half:]


def _combine(y, gates, x1, g2, fg, tm):
    t, d = x1.shape
    nb = t // tm
    vec = pl.BlockSpec((1, d), lambda i: (0, 0))
    yspecs = [pl.BlockSpec((tm, d // 2), functools.partial(lambda i, k: (k * nb + i, 0), k=k))
              for k in range(TOP_K)]
    return pl.pallas_call(
        _combine_kernel,
        grid=(nb,),
        in_specs=yspecs + [pl.BlockSpec((tm, LANES), lambda i: (i, 0)),
                           pl.BlockSpec((tm, d), lambda i: (i, 0)), vec, vec],
        out_specs=pl.BlockSpec((tm, d), lambda i: (i, 0)),
        out_shape=jax.ShapeDtypeStruct((t, d), F32),
        compiler_params=_cparams(("arbitrary",)),
        name="combine",
    )(y, y, y, y, gates, x1, g2, fg)


def kernel(x, c, w_ada, b_ada, norm1_g, w_in_proj, conv_w, conv_b, dt_bias, a_log, d_skip, ssd_norm_g, w_pool,
           b_pool, pool_scale, w_out_proj, norm2_g, w_router, b_router, w_exp_in, b_exp_in, w_exp_out, b_exp_out,
           final_norm_g):
    assert x.shape[0] == 1 and w_ada.shape[0] == 1
    _, t, d = x.shape
    n_experts = w_router.shape[2]
    x1, h2p, idx, gates, g2 = _first_half(x, c, w_ada, b_ada, norm1_g, w_in_proj, conv_w, conv_b, dt_bias, a_log,
                                          d_skip, ssd_norm_g, w_pool, b_pool, pool_scale, w_out_proj, norm2_g,
                                          w_router, b_router)
    row_tok, row_dst, st_e, st_n, st_start, n_super = _route_tables(idx[:, :TOP_K], n_experts)
    f = w_exp_out.shape[2]
    nf = f // MOE_TF
    b_in = b_exp_in[0].reshape(n_experts, nf, MOE_TF, 2)
    b_in_c = jnp.concatenate([b_in[..., 0], b_in[..., 1]], axis=-1)
    src = jnp.concatenate([2 * jnp.arange(LANES), 2 * jnp.arange(LANES) + 1])
    perm = (jnp.arange(2 * LANES)[:, None] == src[None, :]).astype(BF16)
    y = _moe(h2p, row_tok, row_dst, st_e, st_n, st_start, n_super, w_exp_in.reshape(w_exp_in.shape[1:]),
             w_exp_out.reshape(w_exp_out.shape[1:]), b_in_c, b_exp_out.reshape(n_experts, d), perm, t)
    out = _combine(y, gates, x1, g2, final_norm_g.reshape(1, d), min(256, t))
    return out.reshape(x.shape)
```
